```python
import jax, jax.numpy as jnp
from jax import lax
import numpy as np

D_MODEL = 2048
BATCH = 2
SEQ = 16384
DEPTH = 4

GRID_W = 64
CTX_LEN = 256
N_MIXERS = 3
D_BRANCH = D_MODEL
NORM_EPS = 1e-6
FOURIER_GROUPS = 8
HEAD_DIM = 64
N_HEADS = D_BRANCH // HEAD_DIM
N_KV_HEADS = 4
GQA_GROUP = N_HEADS // N_KV_HEADS
WINDOW = 128
BLOCK = 128
BAND = 3 * BLOCK
ROPE_BASE = 10000.0
ATTN_SCALE = HEAD_DIM ** -0.5
NEG_INF = -1e30
ATTN_IN = (N_HEADS + 2 * N_KV_HEADS) * HEAD_DIM + D_BRANCH
CONV_WIDTH = 31
N_FOURIER_LAYERS = (DEPTH + 2) // 3
N_ATTN_LAYERS = (DEPTH + 1) // 3
N_CONV_LAYERS = DEPTH // 3

kernel_name = 'hybrid_fourier_window_conformer_dit'


def rms_norm(x, g):
    xf = x.astype(jnp.float32)
    y = xf * lax.rsqrt(jnp.mean(xf * xf, axis=-1, keepdims=True) + NORM_EPS)
    return (y * g.astype(jnp.float32)).astype(x.dtype)


def layer_norm(x, g, b):
    xf = x.astype(jnp.float32)
    mu = jnp.mean(xf, axis=-1, keepdims=True)
    var = jnp.mean(jnp.square(xf - mu), axis=-1, keepdims=True)
    y = (xf - mu) * lax.rsqrt(var + NORM_EPS) * g.astype(jnp.float32) + b.astype(jnp.float32)
    return y.astype(x.dtype)


def ada_modulation(cond, w, b):
    m = jax.nn.silu(cond) @ w + b
    return jnp.split(m, 3, axis=-1)


def axial_rope_tables(n_tokens):
    rows = n_tokens // GRID_W
    row = jnp.repeat(jnp.arange(rows), GRID_W).astype(jnp.float32)
    col = jnp.tile(jnp.arange(GRID_W), rows).astype(jnp.float32)
    quarter = HEAD_DIM // 4
    inv_freq = ROPE_BASE ** (-jnp.arange(quarter, dtype=jnp.float32) / quarter)
    ang_r = row[:, None] * inv_freq[None, :]
    ang_c = col[:, None] * inv_freq[None, :]
    ang = jnp.concatenate([ang_r, ang_r, ang_c, ang_c], axis=-1)
    return jnp.cos(ang), jnp.sin(ang)


def apply_rope(x, cos, sin):
    n = x.shape[1]
    bshape = (1, n) + (1,) * (x.ndim - 3) + (HEAD_DIM,)
    xf = x.astype(jnp.float32)
    xr = xf.reshape(x.shape[:-1] + (2, 2, HEAD_DIM // 4))
    rot = jnp.stack([-xr[..., 1, :], xr[..., 0, :]], axis=-2).reshape(x.shape)
    return (xf * cos.reshape(bshape) + rot * sin.reshape(bshape)).astype(x.dtype)


def fourier_mix(h, w_in, w_out):
    b, n, _ = h.shape
    u, z = jnp.split(h @ w_in, 2, axis=-1)
    ug = u.reshape(b, n, FOURIER_GROUPS, D_BRANCH // FOURIER_GROUPS).astype(jnp.float32)
    mixed = jnp.fft.fft2(ug, axes=(1, 3), norm='ortho').real
    y = mixed.reshape(b, n, D_BRANCH).astype(h.dtype) * jax.nn.silu(z)
    return y @ w_out


def conformer_conv_mix(h, w_in, dw_w, dw_b, ln_g, ln_b, w_out):
    a, a_gate, z = jnp.split(h @ w_in, 3, axis=-1)
    g = a * jax.nn.sigmoid(a_gate)
    pad = CONV_WIDTH // 2
    y = lax.conv_general_dilated(
        g, dw_w[:, None, :].astype(g.dtype), window_strides=(1,),
        padding=[(pad, pad)], dimension_numbers=('NWC', 'WIO', 'NWC'),
        feature_group_count=D_BRANCH) + dw_b
    y = jax.nn.silu(layer_norm(y, ln_g, ln_b))
    return (y * jax.nn.silu(z)) @ w_out


def split_qkvz(h, w_in):
    b, n, _ = h.shape
    proj = h @ w_in
    q_end = N_HEADS * HEAD_DIM
    k_end = q_end + N_KV_HEADS * HEAD_DIM
    v_end = k_end + N_KV_HEADS * HEAD_DIM
    q = proj[..., :q_end].reshape(b, n, N_KV_HEADS, GQA_GROUP, HEAD_DIM)
    k = proj[..., q_end:k_end].reshape(b, n, N_KV_HEADS, HEAD_DIM)
    v = proj[..., k_end:v_end].reshape(b, n, N_KV_HEADS, HEAD_DIM)
    z = proj[..., v_end:]
    return q, k, v, z


def sink_softmax(s, sink):
    s_sink = jnp.broadcast_to(sink[None, :, :, None, None], s.shape[:-1] + (1,))
    p = jax.nn.softmax(jnp.concatenate([s, s_sink], axis=-1), axis=-1)
    return p[..., :-1]


def window_attention_mix(h, hc, w_in, sink, w_out, cos, sin, with_ctx_out):
    b, n, _ = h.shape
    n_ctx = hc.shape[1]
    n_blocks = n // BLOCK
    q, k, v, z = split_qkvz(h, w_in)
    qc, kc, vc, zc = split_qkvz(hc, w_in)
    q = apply_rope(q, cos, sin) * ATTN_SCALE
    k = apply_rope(k, cos, sin)
    sink = sink.reshape(N_KV_HEADS, GQA_GROUP).astype(jnp.float32)
    pad = ((0, 0), (BLOCK, BLOCK), (0, 0), (0, 0))
    kp = jnp.pad(k, pad)
    vp = jnp.pad(v, pad)
    q_off = jnp.arange(BLOCK)[:, None]
    k_off = jnp.arange(BAND)[None, :] - BLOCK
    rel_ok = jnp.abs(q_off - k_off) <= WINDOW

    def block(bi):
        start = bi * BLOCK
        qb = lax.dynamic_slice_in_dim(q, start, BLOCK, axis=1)
        kb = lax.dynamic_slice_in_dim(kp, start, BAND, axis=1)
        vb = lax.dynamic_slice_in_dim(vp, start, BAND, axis=1)
        kpos = start + k_off
        valid = rel_ok & (kpos >= 0) & (kpos < n)
        s_loc = jnp.einsum('bqhgd,bkhd->bhgqk', qb, kb).astype(jnp.float32)
        s_loc = jnp.where(valid, s_loc, NEG_INF)
        s_ctx = jnp.einsum('bqhgd,bkhd->bhgqk', qb, kc).astype(jnp.float32)
        p = sink_softmax(jnp.concatenate([s_loc, s_ctx], axis=-1), sink).astype(v.dtype)
        return (jnp.einsum('bhgqk,bkhd->bqhgd', p[..., :BAND], vb)
                + jnp.einsum('bhgqk,bkhd->bqhgd', p[..., BAND:], vc))

    o = lax.map(block, jnp.arange(n_blocks))
    o = jnp.moveaxis(o, 0, 1).reshape(b, n, D_BRANCH)
    out = (o * jax.nn.silu(z)) @ w_out
    if not with_ctx_out:
        return out, None
    s_c = jnp.einsum('bqhgd,bkhd->bhgqk', qc * ATTN_SCALE, kc).astype(jnp.float32)
    p_c = sink_softmax(s_c, sink).astype(vc.dtype)
    oc = jnp.einsum('bhgqk,bkhd->bqhgd', p_c, vc).reshape(b, n_ctx, D_BRANCH)
    out_c = (oc * jax.nn.silu(zc)) @ w_out
    return out, out_c


def setup_inputs(seed: int = 0) -> dict:
    key = jax.random.key(seed)
    ks = jax.random.split(key, 20)
    f32 = jnp.float32

    def nrm(k, shape, s):
        return jax.random.normal(k, shape, f32) * s

    d, e = D_MODEL, D_BRANCH
    return {
        'x': nrm(ks[0], (BATCH, SEQ, d), 1.0),
        'c': nrm(ks[1], (BATCH, d), 1.0),
        'ctx': nrm(ks[2], (BATCH, CTX_LEN, d), 1.0),
        'c_ctx': nrm(ks[3], (d,), 1.0),
        'norm_g': 1.0 + nrm(ks[4], (DEPTH, d), 0.05),
        'ada_w': nrm(ks[5], (DEPTH, d, 3 * d), 0.5 * d ** -0.5),
        'ada_b': nrm(ks[6], (DEPTH, 3 * d), 0.02),
        'four_w_in': nrm(ks[7], (N_FOURIER_LAYERS, d, 2 * e), d ** -0.5),
        'four_w_out': nrm(ks[8], (N_FOURIER_LAYERS, e, d), e ** -0.5),
        'attn_w_in': nrm(ks[9], (N_ATTN_LAYERS, d, ATTN_IN), d ** -0.5),
        'attn_sink': nrm(ks[10], (N_ATTN_LAYERS, N_HEADS), 0.5),
        'attn_w_out': nrm(ks[11], (N_ATTN_LAYERS, e, d), e ** -0.5),
        'conv_w_in': nrm(ks[12], (N_CONV_LAYERS, d, 3 * e), d ** -0.5),
        'conv_dw_w': nrm(ks[13], (N_CONV_LAYERS, CONV_WIDTH, e), CONV_WIDTH ** -0.5),
        'conv_dw_b': nrm(ks[14], (N_CONV_LAYERS, e), 0.02),
        'conv_ln_g': 1.0 + nrm(ks[15], (N_CONV_LAYERS, e), 0.05),
        'conv_ln_b': nrm(ks[16], (N_CONV_LAYERS, e), 0.02),
        'conv_w_out': nrm(ks[17], (N_CONV_LAYERS, e, d), e ** -0.5),
        'final_g': 1.0 + nrm(ks[18], (d,), 0.05),
    }


def reference(x, c, ctx, c_ctx, norm_g, ada_w, ada_b, four_w_in, four_w_out,
              attn_w_in, attn_sink, attn_w_out, conv_w_in, conv_dw_w, conv_dw_b,
              conv_ln_g, conv_ln_b, conv_w_out, final_g):
    n_tokens = x.shape[1]
    cos, sin = axial_rope_tables(n_tokens)
    for i in range(DEPTH):
        kind, j = i % N_MIXERS, i // N_MIXERS
        with_ctx = i < DEPTH - 1
        shift, scale, gate = ada_modulation(c, ada_w[i], ada_b[i])
        h = rms_norm(x, norm_g[i]) * (1.0 + scale[:, None, :]) + shift[:, None, :]
        shift_c, scale_c, gate_c = ada_modulation(c_ctx, ada_w[i], ada_b[i])
        hc = rms_norm(ctx, norm_g[i]) * (1.0 + scale_c) + shift_c
        if kind == 0:
            o = fourier_mix(h, four_w_in[j], four_w_out[j])
            oc = fourier_mix(hc, four_w_in[j], four_w_out[j]) if with_ctx else None
        elif kind == 1:
            o, oc = window_attention_mix(h, hc, attn_w_in[j], attn_sink[j], attn_w_out[j],
                                         cos, sin, with_ctx)
        else:
            conv_args = (conv_w_in[j], conv_dw_w[j], conv_dw_b[j], conv_ln_g[j],
                         conv_ln_b[j], conv_w_out[j])
            o = conformer_conv_mix(h, *conv_args)
            oc = conformer_conv_mix(hc, *conv_args) if with_ctx else None
        x = x + gate[:, None, :] * o
        if with_ctx:
            ctx = ctx + gate_c * oc
    return rms_norm(x, final_g)
```

```python
import functools
import math

import jax
import jax.numpy as jnp
from jax import lax
from jax.experimental import pallas as pl
from jax.experimental.pallas import tpu as pltpu

HEAD_DIM = 64
N_KV_HEADS = 4
FOURIER_GROUPS = 8
WINDOW = 128
GRID_W = 64
N_MIXERS = 3
NORM_EPS = 1e-6
ROPE_BASE = 10000.0
NEG_INF = -1e30
ATTN_SCALE = HEAD_DIM ** -0.5

LANES = 128
SUBLANES = 8
VMEM_LIMIT_BYTES = 56 * 1024 * 1024

DFT_N1 = 128
ROW_CHUNK = 16
CONV_HALO = 16

F32 = jnp.float32
BF16 = jnp.bfloat16


def _params(*sem):
    return pltpu.CompilerParams(dimension_semantics=sem, vmem_limit_bytes=VMEM_LIMIT_BYTES)


def _silu(v):
    return v * jax.nn.sigmoid(v)


def _pick_tile(n, cap, quantum):
    if n <= cap:
        return n
    t = cap - cap % quantum
    while n % t:
        t -= quantum
    return t


def _mod_kernel(cond_ref, w_ref, b_ref, o_ref):
    s = _silu(cond_ref[...]).astype(BF16)
    o_ref[0] = jnp.dot(s, w_ref[0].astype(BF16), preferred_element_type=F32) + b_ref[0]


def _modulation(cond, ada_w, ada_b):
    depth, d, d3 = ada_w.shape
    r = cond.shape[0]
    tn = _pick_tile(d3, 1024, 2 * LANES)
    return pl.pallas_call(
        _mod_kernel,
        grid=(depth, d3 // tn),
        in_specs=[
            pl.BlockSpec((r, d), lambda l, j: (0, 0)),
            pl.BlockSpec((1, d, tn), lambda l, j: (l, 0, j)),
            pl.BlockSpec((1, 1, tn), lambda l, j: (l, 0, j)),
        ],
        out_specs=pl.BlockSpec((1, r, tn), lambda l, j: (l, 0, j)),
        out_shape=jax.ShapeDtypeStruct((depth, r, d3), F32),
        compiler_params=_params("arbitrary", "arbitrary"),
        name="ada_modulation",
    )(cond, ada_w, ada_b.reshape(depth, 1, d3))


def _in_proj_kernel(x_ref, g_ref, shift_ref, scale_ref, w_ref, o_ref, h_scr):
    tm = x_ref.shape[0]

    @pl.when(pl.program_id(1) == 0)
    def _():
        g = g_ref[...]
        sh = shift_ref[0]
        sc = 1.0 + scale_ref[0]

        def body(r, carry):
            rows = pl.ds(pl.multiple_of(r * ROW_CHUNK, ROW_CHUNK), ROW_CHUNK)
            xf = x_ref[rows, :]
            ms = jnp.mean(xf * xf, axis=-1, keepdims=True)
            y = xf * lax.rsqrt(ms + NORM_EPS)
            h_scr[rows, :] = ((y * g) * sc + sh).astype(BF16)
            return carry

        lax.fori_loop(0, tm // ROW_CHUNK, body, 0)

    o_ref[...] = jnp.dot(h_scr[...], w_ref[...], preferred_element_type=F32).astype(BF16)


def _in_proj(x2, norm_g, shift, scale, w_bf16, rows_per_mod, mod_row0):
    t, d = x2.shape
    n_out = w_bf16.shape[1]
    tm = min(1024, rows_per_mod, t)
    tn = _pick_tile(n_out, 1024, 2 * LANES)
    assert t % tm == 0 and rows_per_mod % tm == 0 and n_out % tn == 0
    tiles_per_mod = rows_per_mod // tm
    mod_map = lambda i, j: (mod_row0 + i // tiles_per_mod, 0, 0)
    return pl.pallas_call(
        _in_proj_kernel,
        grid=(t // tm, n_out // tn),
        in_specs=[
            pl.BlockSpec((tm, d), lambda i, j: (i, 0)),
            pl.BlockSpec((1, d), lambda i, j: (0, 0)),
            pl.BlockSpec((1, 1, d), mod_map),
            pl.BlockSpec((1, 1, d), mod_map),
            pl.BlockSpec((d, tn), lambda i, j: (0, j)),
        ],
        out_specs=pl.BlockSpec((tm, tn), lambda i, j: (i, j)),
        out_shape=jax.ShapeDtypeStruct((t, n_out), BF16),
        scratch_shapes=[pltpu.VMEM((tm, d), BF16)],
        compiler_params=_params("arbitrary", "arbitrary"),
        name="norm_in_proj",
    )(x2, norm_g.reshape(1, d), shift, scale, w_bf16)


def _out_proj_kernel(a_ref, z_ref, x_ref, gate_ref, w_ref, fg_ref, o_ref, y_scr, *, final_norm):
    tm = x_ref.shape[0]
    n_chunks = tm // ROW_CHUNK

    def gate_body(r, carry):
        rows = pl.ds(pl.multiple_of(r * ROW_CHUNK, ROW_CHUNK), ROW_CHUNK)
        y_scr[rows, :] = (a_ref[rows, :].astype(F32) * _silu(z_ref[rows, :].astype(F32))).astype(BF16)
        return carry

    lax.fori_loop(0, n_chunks, gate_body, 0)
    o_ref[...] = jnp.dot(y_scr[...], w_ref[...], preferred_element_type=F32)
    gate = gate_ref[0]
    fg = fg_ref[...]

    def res_body(r, carry):
        rows = pl.ds(pl.multiple_of(r * ROW_CHUNK, ROW_CHUNK), ROW_CHUNK)
        xn = x_ref[rows, :] + gate * o_ref[rows, :]
        if final_norm:
            ms = jnp.mean(xn * xn, axis=-1, keepdims=True)
            xn = (xn * lax.rsqrt(ms + NORM_EPS)) * fg
        o_ref[rows, :] = xn
        return carry

    lax.fori_loop(0, n_chunks, res_body, 0)


def _out_proj(a, z_arr, z_col, x2, gate, w_bf16, final_g, rows_per_mod, mod_row0, final_norm):
    t, d = x2.shape
    tm = min(512, rows_per_mod, t)
    assert t % tm == 0 and rows_per_mod % tm == 0
    tiles_per_mod = rows_per_mod // tm
    return pl.pallas_call(
        functools.partial(_out_proj_kernel, final_norm=final_norm),
        grid=(t // tm,),
        in_specs=[
            pl.BlockSpec((tm, d), lambda i: (i, 0)),
            pl.BlockSpec((tm, d), lambda i: (i, z_col)),
            pl.BlockSpec((tm, d), lambda i: (i, 0)),
            pl.BlockSpec((1, 1, d), lambda i: (mod_row0 + i // tiles_per_mod, 0, 0)),
            pl.BlockSpec((d, d), lambda i: (0, 0)),
            pl.BlockSpec((1, d), lambda i: (0, 0)),
        ],
        out_specs=pl.BlockSpec((tm, d), lambda i: (i, 0)),
        out_shape=jax.ShapeDtypeStruct((t, d), F32),
        scratch_shapes=[pltpu.VMEM((tm, d), BF16)],
        compiler_params=_params("arbitrary"),
        name="gated_out_proj",
    )(a, z_arr, x2, gate, w_bf16, final_g.reshape(1, d))


def _cos_sin(n_rows, n_cols, period):
    r = jnp.arange(n_rows, dtype=jnp.int32)[:, None]
    c = jnp.arange(n_cols, dtype=jnp.int32)[None, :]
    ang = ((r * c) % period).astype(F32) * (2.0 * math.pi / period)
    return jnp.cos(ang), jnp.sin(ang)


def _chan_dft_kernel(u_ref, m_ref, z_ref, *, groups):
    gs = m_ref.shape[0]
    for g in range(groups):
        cols = slice(g * gs, (g + 1) * gs)
        r = jnp.dot(u_ref[:, cols], m_ref[...], preferred_element_type=F32)
        z_ref[0, :, cols] = r[:, :gs].astype(BF16)
        z_ref[1, :, cols] = r[:, gs:].astype(BF16)


def _chan_dft(proj, d):
    t = proj.shape[0]
    gs = d // FOURIER_GROUPS
    c, s = _cos_sin(gs, gs, gs)
    m = jnp.concatenate([c, -s], axis=1).astype(BF16)
    tm = min(1024, t)
    assert t % tm == 0
    return pl.pallas_call(
        functools.partial(_chan_dft_kernel, groups=FOURIER_GROUPS),
        grid=(t // tm,),
        in_specs=[
            pl.BlockSpec((tm, d), lambda i: (i, 0)),
            pl.BlockSpec((gs, 2 * gs), lambda i: (0, 0)),
        ],
        out_specs=pl.BlockSpec((2, tm, d), lambda i: (0, i, 0)),
        out_shape=jax.ShapeDtypeStruct((2, t, d), BF16),
        compiler_params=_params("arbitrary"),
        name="fourier_chan_dft",
    )(proj, m)


def _seq_dft1_kernel(z_ref, m_ref, twc_ref, tws_ref, y_ref, *, d):
    n1 = z_ref.shape[2]
    ct = z_ref.shape[3]
    zin = jnp.concatenate([z_ref[0, 0], z_ref[1, 0]], axis=0)
    y = jnp.dot(m_ref[...], zin, preferred_element_type=F32)
    twc = twc_ref[0]
    tws = tws_ref[0]
    for q in range(ct // d):
        cols = slice(q * d, (q + 1) * d)
        yr = y[:n1, cols]
        yi = y[n1:, cols]
        c = twc[:, q:q + 1]
        s = tws[:, q:q + 1]
        y_ref[0, 0, :, cols] = (yr * c + yi * s).astype(BF16)
        y_ref[1, 0, :, cols] = (yi * c - yr * s).astype(BF16)


def _seq_dft1(z, b, seq, d):
    n1 = DFT_N1
    n2 = seq // n1
    n2t = min(4, n2)
    assert seq % n1 == 0 and n2 % n2t == 0
    nj = n2 // n2t
    c1, s1 = _cos_sin(n1, n1, n1)
    m = jnp.concatenate([jnp.concatenate([c1, s1], axis=1),
                         jnp.concatenate([-s1, c1], axis=1)], axis=0).astype(BF16)
    twc, tws = _cos_sin(n1, n2, seq)
    twc = twc.reshape(n1, nj, n2t).transpose(1, 0, 2)
    tws = tws.reshape(n1, nj, n2t).transpose(1, 0, 2)
    zv = z.reshape(2, b, n1, n2 * d)
    ct = n2t * d
    y = pl.pallas_call(
        functools.partial(_seq_dft1_kernel, d=d),
        grid=(b, nj),
        in_specs=[
            pl.BlockSpec((2, 1, n1, ct), lambda bi, j: (0, bi, 0, j)),
            pl.BlockSpec((2 * n1, 2 * n1), lambda bi, j: (0, 0)),
            pl.BlockSpec((1, n1, n2t), lambda bi, j: (j, 0, 0)),
            pl.BlockSpec((1, n1, n2t), lambda bi, j: (j, 0, 0)),
        ],
        out_specs=pl.BlockSpec((2, 1, n1, ct), lambda bi, j: (0, bi, 0, j)),
        out_shape=jax.ShapeDtypeStruct((2, b, n1, n2 * d), BF16),
        compiler_params=_params("arbitrary", "arbitrary"),
        name="fourier_seq_dft_stage1",
    )(zv, m, twc, tws)
    return y.reshape(2, b, n1, n2, d)


def _seq_dft2_kernel(y_ref, m_ref, o_ref, *, d, inv_norm):
    k1t = y_ref.shape[2]
    for kk in range(k1t):
        yin = jnp.concatenate([y_ref[0, 0, kk], y_ref[1, 0, kk]], axis=0)
        r = jnp.dot(m_ref[...], yin, preferred_element_type=F32) * inv_norm
        o_ref[0, :, kk * d:(kk + 1) * d] = r.astype(BF16)


def _seq_dft2(y, inv_norm):
    _, b, k1n, n2, d = y.shape
    k1t = min(4, k1n)
    assert k1n % k1t == 0
    c2, s2 = _cos_sin(n2, n2, n2)
    m = jnp.concatenate([c2, s2], axis=1).astype(BF16)
    out = pl.pallas_call(
        functools.partial(_seq_dft2_kernel, d=d, inv_norm=inv_norm),
        grid=(b, k1n // k1t),
        in_specs=[
            pl.BlockSpec((2, 1, k1t, n2, d), lambda bi, k: (0, bi, k, 0, 0)),
            pl.BlockSpec((n2, 2 * n2), lambda bi, k: (0, 0)),
        ],
        out_specs=pl.BlockSpec((1, n2, k1t * d), lambda bi, k: (bi, 0, k)),
        out_shape=jax.ShapeDtypeStruct((b, n2, k1n * d), BF16),
        compiler_params=_params("arbitrary", "arbitrary"),
        name="fourier_seq_dft_stage2",
    )(y, m)
    return out.reshape(b * n2 * k1n, d)


def _fourier_mixed(proj, b, seq, d, two_stage):
    z = _chan_dft(proj, d)
    inv_norm = 1.0 / math.sqrt(seq * (d // FOURIER_GROUPS))
    if two_stage:
        y = _seq_dft1(z, b, seq, d)
    else:
        y = z.reshape(2, b, 1, seq, d)
    return _seq_dft2(y, inv_norm)


def _rope_tables(n_tokens):
    rows = n_tokens // GRID_W
    row = jnp.repeat(jnp.arange(rows), GRID_W).astype(F32)
    col = jnp.tile(jnp.arange(GRID_W), rows).astype(F32)
    quarter = HEAD_DIM // 4
    inv_freq = ROPE_BASE ** (-jnp.arange(quarter, dtype=F32) / quarter)
    ang_r = row[:, None] * inv_freq[None, :]
    ang_c = col[:, None] * inv_freq[None, :]
    ang = jnp.concatenate([ang_r, ang_r, ang_c, ang_c], axis=-1)
    sign = jnp.where((jnp.arange(HEAD_DIM) % (HEAD_DIM // 2)) < quarter, -1.0, 1.0).astype(F32)
    reps = LANES // HEAD_DIM
    return jnp.tile(jnp.cos(ang), (1, reps)), jnp.tile(jnp.sin(ang) * sign, (1, reps))


def _rope128(xf, cos, sin_signed):
    quarter = HEAD_DIM // 4
    lane = lax.broadcasted_iota(jnp.int32, xf.shape, 1)
    from_hi = pltpu.roll(xf, LANES - quarter, axis=1)
    from_lo = pltpu.roll(xf, quarter, axis=1)
    partner = jnp.where((lane % (HEAD_DIM // 2)) < quarter, from_hi, from_lo)
    return xf * cos + partner * sin_signed


def _k_rope_kernel(k_ref, cos_ref, sin_ref, o_ref):
    cos = cos_ref[...]
    sin = sin_ref[...]
    for c in range(k_ref.shape[1] // LANES):
        cols = slice(c * LANES, (c + 1) * LANES)
        o_ref[:, cols] = _rope128(k_ref[:, cols].astype(F32), cos, sin).astype(BF16)


def _k_rope(proj, k_col, kvw, seq, cos, sin):
    t = proj.shape[0]
    tm = min(1024, seq)
    assert seq % tm == 0 and kvw % LANES == 0
    tiles_per_seq = seq // tm
    return pl.pallas_call(
        _k_rope_kernel,
        grid=(t // tm,),
        in_specs=[
            pl.BlockSpec((tm, kvw), lambda i: (i, k_col)),
            pl.BlockSpec((tm, LANES), lambda i: (i % tiles_per_seq, 0)),
            pl.BlockSpec((tm, LANES), lambda i: (i % tiles_per_seq, 0)),
        ],
        out_specs=pl.BlockSpec((tm, kvw), lambda i: (i, 0)),
        out_shape=jax.ShapeDtypeStruct((t, kvw), BF16),
        compiler_params=_params("arbitrary"),
        name="attn_k_rope",
    )(proj, cos, sin)


def _head_halves(cat_f32, lo):
    lane = lax.broadcasted_iota(jnp.int32, cat_f32.shape, 1)
    same = jnp.where((lane >= lo) & (lane < lo + HEAD_DIM), cat_f32, 0.0)
    swap = pltpu.roll(same, HEAD_DIM, axis=1)
    low, high = (same, swap) if lo == 0 else (swap, same)
    return low.astype(BF16), high.astype(BF16)


def _attn_kernel(*refs, band, bq, seq, n_ctx, groups):
    if band:
        (sink_ref, q_ref, cos_ref, sin_ref, kp_ref, kc_ref, kn_ref,
         vp_ref, vc_ref, vn_ref, kx_ref, vx_ref, o_ref) = refs
    else:
        sink_ref, q_ref, kx_ref, vx_ref, o_ref = refs
    n_band = bq + 2 * WINDOW if band else 0
    nk = n_band + n_ctx
    if band:
        start = pl.program_id(1) * bq
        qpos = start + lax.broadcasted_iota(jnp.int32, (bq, nk), 0)
        col = lax.broadcasted_iota(jnp.int32, (bq, nk), 1)
        kpos = start - WINDOW + col
        valid = ((jnp.abs(qpos - kpos) <= WINDOW) & (kpos >= 0) & (kpos < seq)) | (col >= n_band)
        cos = cos_ref[...]
        sin = sin_ref[...]
    lane_q = lax.broadcasted_iota(jnp.int32, (bq, LANES), 1)

    def probs(s, sink):
        if band:
            s = jnp.where(valid, s, NEG_INF)
        m = jnp.maximum(jnp.max(s, axis=-1, keepdims=True), sink)
        p = jnp.exp(s - m)
        denom = jnp.sum(p, axis=-1, keepdims=True) + jnp.exp(sink - m)
        return p.astype(BF16), 1.0 / denom

    for h in range(N_KV_HEADS):
        pair_cols = slice((h // 2) * LANES, (h // 2 + 1) * LANES)
        if band:
            k_parts = [kp_ref[:, pair_cols], kc_ref[:, pair_cols], kn_ref[:, pair_cols], kx_ref[:, pair_cols]]
            v_parts = [vp_ref[:, pair_cols], vc_ref[:, pair_cols], vn_ref[:, pair_cols], vx_ref[:, pair_cols]]
            kcat = jnp.concatenate([p.astype(F32) for p in k_parts], axis=0)
            vcat = jnp.concatenate([p.astype(F32) for p in v_parts], axis=0)
        else:
            kcat = kx_ref[:, pair_cols].astype(F32)
            vcat = vx_ref[:, pair_cols].astype(F32)
        lo = (h % 2) * HEAD_DIM
        k_lo, k_hi = _head_halves(kcat, lo)
        v_lo, v_hi = _head_halves(vcat, lo)
        for p in range(groups // 2):
            off = (h * groups + 2 * p) * HEAD_DIM
            cols = slice(off, off + LANES)
            qf = q_ref[:, cols].astype(F32)
            if band:
                qf = _rope128(qf, cos, sin)
            qb = (qf * ATTN_SCALE).astype(BF16)
            nt = (((1,), (1,)), ((), ()))
            s0 = lax.dot_general(qb, k_lo, nt, preferred_element_type=F32)
            s1 = lax.dot_general(qb, k_hi, nt, preferred_element_type=F32)
            p0, inv0 = probs(s0, sink_ref[h * groups + 2 * p])
            p1, inv1 = probs(s1, sink_ref[h * groups + 2 * p + 1])
            o_pair = (jnp.dot(p0, v_lo, preferred_element_type=F32)
                      + jnp.dot(p1, v_hi, preferred_element_type=F32))
            o_ref[:, cols] = (o_pair * jnp.where(lane_q < HEAD_DIM, inv0, inv1)).astype(BF16)


def _attention(sink, proj, q_col, v_col, k_rot, proj_c, kc_col, vc_col, b, seq, n_ctx, d, cos, sin):
    kvw = N_KV_HEADS * HEAD_DIM
    groups = d // kvw
    bq = min(256, seq)
    assert seq % bq == 0 and bq % WINDOW == 0 and groups % 2 == 0 and kvw % LANES == 0
    nq = seq // bq
    wpb = bq // WINDOW
    n_halo = b * seq // WINDOW

    def prev_map(col):
        return lambda bi, i: (jnp.maximum((bi * nq + i) * wpb - 1, 0), col)

    def next_map(col):
        return lambda bi, i: (jnp.minimum((bi * nq + i + 1) * wpb, n_halo - 1), col)

    def cur_map(col):
        return lambda bi, i: (bi * nq + i, col)

    return pl.pallas_call(
        functools.partial(_attn_kernel, band=True, bq=bq, seq=seq, n_ctx=n_ctx, groups=groups),
        grid=(b, nq),
        in_specs=[
            pl.BlockSpec(memory_space=pltpu.SMEM),
            pl.BlockSpec((bq, d), cur_map(q_col)),
            pl.BlockSpec((bq, LANES), lambda bi, i: (i, 0)),
            pl.BlockSpec((bq, LANES), lambda bi, i: (i, 0)),
            pl.BlockSpec((WINDOW, kvw), prev_map(0)),
            pl.BlockSpec((bq, kvw), cur_map(0)),
            pl.BlockSpec((WINDOW, kvw), next_map(0)),
            pl.BlockSpec((WINDOW, kvw), prev_map(v_col)),
            pl.BlockSpec((bq, kvw), cur_map(v_col)),
            pl.BlockSpec((WINDOW, kvw), next_map(v_col)),
            pl.BlockSpec((n_ctx, kvw), lambda bi, i: (bi, kc_col)),
            pl.BlockSpec((n_ctx, kvw), lambda bi, i: (bi, vc_col)),
        ],
        out_specs=pl.BlockSpec((bq, d), lambda bi, i: (bi * nq + i, 0)),
        out_shape=jax.ShapeDtypeStruct((b * seq, d), BF16),
        compiler_params=_params("arbitrary", "arbitrary"),
        name="window_attention",
    )(sink, proj, cos, sin, k_rot, k_rot, k_rot, proj, proj, proj, proj_c, proj_c)


def _ctx_attention(sink, proj_c, q_col, k_col, v_col, b, n_ctx, d):
    kvw = N_KV_HEADS * HEAD_DIM
    groups = d // kvw
    return pl.pallas_call(
        functools.partial(_attn_kernel, band=False, bq=n_ctx, seq=n_ctx, n_ctx=n_ctx, groups=groups),
        grid=(b,),
        in_specs=[
            pl.BlockSpec(memory_space=pltpu.SMEM),
            pl.BlockSpec((n_ctx, d), lambda bi: (bi, q_col)),
            pl.BlockSpec((n_ctx, kvw), lambda bi: (bi, k_col)),
            pl.BlockSpec((n_ctx, kvw), lambda bi: (bi, v_col)),
        ],
        out_specs=pl.BlockSpec((n_ctx, d), lambda bi: (bi, 0)),
        out_shape=jax.ShapeDtypeStruct((b * n_ctx, d), BF16),
        compiler_params=_params("arbitrary"),
        name="context_attention",
    )(sink, proj_c, proj_c, proj_c)


def _conv_kernel(a_ref, ag_ref, ap_ref, agp_ref, an_ref, agn_ref, w_ref, b_ref, lg_ref, lb_ref,
                 o_ref, g_scr, sh_scr, c_scr, *, tiles_per_seq, kw):
    tm, d = a_ref.shape
    halo = CONV_HALO
    span = tm + 2 * halo
    i = pl.program_id(0)
    first = (i % tiles_per_seq) == 0
    last = (i % tiles_per_seq) == tiles_per_seq - 1

    def glu(a, ag):
        return a.astype(F32) * jax.nn.sigmoid(ag.astype(F32))

    g_scr[0:halo, :] = jnp.where(first, 0.0, glu(ap_ref[...], agp_ref[...]))

    def glu_body(r, carry):
        rows = pl.ds(pl.multiple_of(r * ROW_CHUNK, ROW_CHUNK), ROW_CHUNK)
        g_scr[pl.ds(pl.multiple_of(halo + r * ROW_CHUNK, ROW_CHUNK), ROW_CHUNK), :] = glu(a_ref[rows, :], ag_ref[rows, :])
        return carry

    lax.fori_loop(0, tm // ROW_CHUNK, glu_body, 0)
    g_scr[halo + tm:span, :] = jnp.where(last, 0.0, glu(an_ref[...], agn_ref[...]))
    g_scr[span:span + SUBLANES, :] = jnp.zeros((SUBLANES, d), F32)

    for r in range(SUBLANES):
        sh_scr[r] = g_scr[r:r + span, :]

    pad = kw // 2
    lw = 4 * LANES
    bias = b_ref[...]
    lg = lg_ref[...]
    lb = lb_ref[...]

    def row_body(rc, carry):
        base = pl.multiple_of(rc * ROW_CHUNK, ROW_CHUNK)
        for ch in range(d // lw):
            cols = slice(ch * lw, (ch + 1) * lw)
            acc = jnp.zeros((ROW_CHUNK, lw), F32)
            for j in range(kw):
                off = j + halo - pad
                rows = pl.ds(pl.multiple_of(base + (off // SUBLANES) * SUBLANES, SUBLANES), ROW_CHUNK)
                acc = acc + w_ref[j:j + 1, cols] * sh_scr[off % SUBLANES, rows, cols]
            c_scr[pl.ds(base, ROW_CHUNK), cols] = acc + bias[:, cols]
        y = c_scr[pl.ds(base, ROW_CHUNK), :]
        mu = jnp.mean(y, axis=-1, keepdims=True)
        var = jnp.mean(jnp.square(y - mu), axis=-1, keepdims=True)
        yn = (y - mu) * lax.rsqrt(var + NORM_EPS) * lg + lb
        o_ref[pl.ds(base, ROW_CHUNK), :] = _silu(yn).astype(BF16)
        return carry

    lax.fori_loop(0, tm // ROW_CHUNK, row_body, 0)


def _conv_module(proj, seq, d, dw_w, dw_b, ln_g, ln_b):
    t = proj.shape[0]
    kw = dw_w.shape[0]
    tm = min(256, seq)
    halo = CONV_HALO
    assert seq % tm == 0 and tm % halo == 0 and kw // 2 <= halo and d % (4 * LANES) == 0
    tiles_per_seq = seq // tm
    hpt = tm // halo
    n_halo = t // halo
    span = tm + 2 * halo

    def prev_map(col):
        return lambda i: (jnp.maximum(i * hpt - 1, 0), col)

    def next_map(col):
        return lambda i: (jnp.minimum((i + 1) * hpt, n_halo - 1), col)

    vec = lambda: pl.BlockSpec((1, d), lambda i: (0, 0))
    return pl.pallas_call(
        functools.partial(_conv_kernel, tiles_per_seq=tiles_per_seq, kw=kw),
        grid=(t // tm,),
        in_specs=[
            pl.BlockSpec((tm, d), lambda i: (i, 0)),
            pl.BlockSpec((tm, d), lambda i: (i, 1)),
            pl.BlockSpec((halo, d), prev_map(0)),
            pl.BlockSpec((halo, d), prev_map(1)),
            pl.BlockSpec((halo, d), next_map(0)),
            pl.BlockSpec((halo, d), next_map(1)),
            pl.BlockSpec((kw, d), lambda i: (0, 0)),
            vec(), vec(), vec(),
        ],
        out_specs=pl.BlockSpec((tm, d), lambda i: (i, 0)),
        out_shape=jax.ShapeDtypeStruct((t, d), BF16),
        scratch_shapes=[
            pltpu.VMEM((span + SUBLANES, d), F32),
            pltpu.VMEM((SUBLANES, span, d), F32),
            pltpu.VMEM((tm, d), F32),
        ],
        compiler_params=_params("arbitrary"),
        name="conformer_conv",
    )(proj, proj, proj, proj, proj, proj, dw_w, dw_b.reshape(1, d), ln_g.reshape(1, d), ln_b.reshape(1, d))


def kernel(x, c, ctx, c_ctx, norm_g, ada_w, ada_b, four_w_in, four_w_out, attn_w_in, attn_sink,
           attn_w_out, conv_w_in, conv_dw_w, conv_dw_b, conv_ln_g, conv_ln_b, conv_w_out, final_g):
    b, seq, d = x.shape
    n_ctx = ctx.shape[1]
    depth = norm_g.shape[0]
    kvw = N_KV_HEADS * HEAD_DIM
    assert d % kvw == 0 and seq % GRID_W == 0

    n_rows = -(-(b + 1) // SUBLANES) * SUBLANES
    cond = jnp.concatenate([c, c_ctx[None, :], jnp.zeros((n_rows - b - 1, d), F32)], axis=0)
    mod = _modulation(cond, ada_w, ada_b)

    cos, sin = _rope_tables(seq)
    qe = d
    ke = qe + kvw
    ve = ke + kvw
    attn_w_perm = jnp.concatenate([attn_w_in[:, :, ve:], attn_w_in[:, :, :ve]], axis=-1)

    x2 = x.reshape(b * seq, d)
    c2 = ctx.reshape(b * n_ctx, d)
    for i in range(depth):
        kind, j = i % N_MIXERS, i // N_MIXERS
        with_ctx = i < depth - 1
        last = i == depth - 1
        shift = mod[i, :, 0:d].reshape(n_rows, 1, d)
        scale = mod[i, :, d:2 * d].reshape(n_rows, 1, d)
        gate = mod[i, :, 2 * d:3 * d].reshape(n_rows, 1, d)
        if kind == 0:
            w_in = four_w_in[j].astype(BF16)
            w_out = four_w_out[j].astype(BF16)
            proj = _in_proj(x2, norm_g[i], shift, scale, w_in, seq, 0)
            a = _fourier_mixed(proj, b, seq, d, two_stage=True)
            z_col = 1
            if with_ctx:
                proj_c = _in_proj(c2, norm_g[i], shift, scale, w_in, b * n_ctx, b)
                a_c = _fourier_mixed(proj_c, b, n_ctx, d, two_stage=False)
        elif kind == 1:
            w_in = attn_w_perm[j].astype(BF16)
            w_out = attn_w_out[j].astype(BF16)
            z_col, q_col = 0, 1
            k_col = 2 * d // kvw
            v_col = k_col + 1
            proj = _in_proj(x2, norm_g[i], shift, scale, w_in, seq, 0)
            proj_c = _in_proj(c2, norm_g[i], shift, scale, w_in, b * n_ctx, b)
            k_rot = _k_rope(proj, k_col, kvw, seq, cos, sin)
            a = _attention(attn_sink[j], proj, q_col, v_col, k_rot, proj_c, k_col, v_col,
                           b, seq, n_ctx, d, cos, sin)
            if with_ctx:
                a_c = _ctx_attention(attn_sink[j], proj_c, q_col, k_col, v_col, b, n_ctx, d)
        else:
            w_in = conv_w_in[j].astype(BF16)
            w_out = conv_w_out[j].astype(BF16)
            z_col = 2
            conv_args = (conv_dw_w[j], conv_dw_b[j], conv_ln_g[j], conv_ln_b[j])
            proj = _in_proj(x2, norm_g[i], shift, scale, w_in, seq, 0)
            a = _conv_module(proj, seq, d, *conv_args)
            if with_ctx:
                proj_c = _in_proj(c2, norm_g[i], shift, scale, w_in, b * n_ctx, b)
                a_c = _conv_module(proj_c, n_ctx, d, *conv_args)
        x2 = _out_proj(a, proj, z_col, x2, gate, w_out, final_g, seq, 0, final_norm=last)
        if with_ctx:
            c2 = _out_proj(a_c, proj_c, z_col, c2, gate, w_out, final_g, b * n_ctx, b, final_norm=False)
    return x2.reshape(b, seq, d)
```

```python
import functools
import math

import jax
import jax.numpy as jnp
from jax import lax
from jax.experimental import pallas as pl
from jax.experimental.pallas import tpu as pltpu

HEAD_DIM = 64
N_KV_HEADS = 4
FOURIER_GROUPS = 8
WINDOW = 128
GRID_W = 64
N_MIXERS = 3
NORM_EPS = 1e-6
ROPE_BASE = 10000.0
NEG_INF = -1e30
ATTN_SCALE = HEAD_DIM ** -0.5

LANES = 128
SUBLANES = 8
VMEM_LIMIT_BYTES = 56 * 1024 * 1024

DFT_N1 = 128
ROW_CHUNK = 16
ROW_UNROLL = 4
CONV_HALO = 16
CONV_ACCS = 2

F32 = jnp.float32
BF16 = jnp.bfloat16


def _params(*sem):
    return pltpu.CompilerParams(dimension_semantics=sem, vmem_limit_bytes=VMEM_LIMIT_BYTES)


def _silu(v):
    return v * jax.nn.sigmoid(v)


def _pick_tile(n, cap, quantum):
    if n <= cap:
        return n
    t = cap - cap % quantum
    while n % t:
        t -= quantum
    return t


def _mod_kernel(cond_ref, w_ref, b_ref, o_ref):
    s = _silu(cond_ref[...]).astype(BF16)
    o_ref[0] = jnp.dot(s, w_ref[0].astype(BF16), preferred_element_type=F32) + b_ref[0]


def _modulation(cond, ada_w, ada_b):
    depth, d, d3 = ada_w.shape
    r = cond.shape[0]
    tn = _pick_tile(d3, 1024, 2 * LANES)
    return pl.pallas_call(
        _mod_kernel,
        grid=(depth, d3 // tn),
        in_specs=[
            pl.BlockSpec((r, d), lambda l, j: (0, 0)),
            pl.BlockSpec((1, d, tn), lambda l, j: (l, 0, j)),
            pl.BlockSpec((1, 1, tn), lambda l, j: (l, 0, j)),
        ],
        out_specs=pl.BlockSpec((1, r, tn), lambda l, j: (l, 0, j)),
        out_shape=jax.ShapeDtypeStruct((depth, r, d3), F32),
        compiler_params=_params("arbitrary", "arbitrary"),
        name="ada_modulation",
    )(cond, ada_w, ada_b.reshape(depth, 1, d3))


def _in_proj_kernel(x_ref, g_ref, shift_ref, scale_ref, w_ref, o_ref, h_scr):
    tm = x_ref.shape[0]

    @pl.when(pl.program_id(1) == 0)
    def _():
        g = g_ref[...]
        sh = shift_ref[0]
        sc = 1.0 + scale_ref[0]

        def body(r, carry):
            rows = pl.ds(pl.multiple_of(r * ROW_CHUNK, ROW_CHUNK), ROW_CHUNK)
            xf = x_ref[rows, :]
            ms = jnp.mean(xf * xf, axis=-1, keepdims=True)
            y = xf * lax.rsqrt(ms + NORM_EPS)
            h_scr[rows, :] = ((y * g) * sc + sh).astype(BF16)
            return carry

        lax.fori_loop(0, tm // ROW_CHUNK, body, 0, unroll=ROW_UNROLL)

    o_ref[...] = jnp.dot(h_scr[...], w_ref[...], preferred_element_type=F32).astype(BF16)


def _in_proj(x2, norm_g, shift, scale, w_bf16, rows_per_mod, mod_row0):
    t, d = x2.shape
    n_out = w_bf16.shape[1]
    tm = min(1024, rows_per_mod, t)
    tn = _pick_tile(n_out, 1024, 2 * LANES)
    assert t % tm == 0 and rows_per_mod % tm == 0 and n_out % tn == 0
    tiles_per_mod = rows_per_mod // tm
    mod_map = lambda i, j: (mod_row0 + i // tiles_per_mod, 0, 0)
    return pl.pallas_call(
        _in_proj_kernel,
        grid=(t // tm, n_out // tn),
        in_specs=[
            pl.BlockSpec((tm, d), lambda i, j: (i, 0)),
            pl.BlockSpec((1, d), lambda i, j: (0, 0)),
            pl.BlockSpec((1, 1, d), mod_map),
            pl.BlockSpec((1, 1, d), mod_map),
            pl.BlockSpec((d, tn), lambda i, j: (0, j)),
        ],
        out_specs=pl.BlockSpec((tm, tn), lambda i, j: (i, j)),
        out_shape=jax.ShapeDtypeStruct((t, n_out), BF16),
        scratch_shapes=[pltpu.VMEM((tm, d), BF16)],
        compiler_params=_params("arbitrary", "arbitrary"),
        name="norm_in_proj",
    )(x2, norm_g.reshape(1, d), shift, scale, w_bf16)


def _out_proj_kernel(a_ref, z_ref, x_ref, gate_ref, w_ref, fg_ref, o_ref, y_scr, *, final_norm):
    tm = x_ref.shape[0]
    n_chunks = tm // ROW_CHUNK

    def gate_body(r, carry):
        rows = pl.ds(pl.multiple_of(r * ROW_CHUNK, ROW_CHUNK), ROW_CHUNK)
        y_scr[rows, :] = (a_ref[rows, :].astype(F32) * _silu(z_ref[rows, :].astype(F32))).astype(BF16)
        return carry

    lax.fori_loop(0, n_chunks, gate_body, 0, unroll=ROW_UNROLL)
    o_ref[...] = jnp.dot(y_scr[...], w_ref[...], preferred_element_type=F32)
    gate = gate_ref[0]
    fg = fg_ref[...]

    def res_body(r, carry):
        rows = pl.ds(pl.multiple_of(r * ROW_CHUNK, ROW_CHUNK), ROW_CHUNK)
        xn = x_ref[rows, :] + gate * o_ref[rows, :]
        if final_norm:
            ms = jnp.mean(xn * xn, axis=-1, keepdims=True)
            xn = (xn * lax.rsqrt(ms + NORM_EPS)) * fg
        o_ref[rows, :] = xn
        return carry

    lax.fori_loop(0, n_chunks, res_body, 0, unroll=ROW_UNROLL)


def _out_proj(a, z_arr, z_col, x2, gate, w_bf16, final_g, rows_per_mod, mod_row0, final_norm):
    t, d = x2.shape
    tm = min(512, rows_per_mod, t)
    assert t % tm == 0 and rows_per_mod % tm == 0
    tiles_per_mod = rows_per_mod // tm
    return pl.pallas_call(
        functools.partial(_out_proj_kernel, final_norm=final_norm),
        grid=(t // tm,),
        in_specs=[
            pl.BlockSpec((tm, d), lambda i: (i, 0)),
            pl.BlockSpec((tm, d), lambda i: (i, z_col)),
            pl.BlockSpec((tm, d), lambda i: (i, 0)),
            pl.BlockSpec((1, 1, d), lambda i: (mod_row0 + i // tiles_per_mod, 0, 0)),
            pl.BlockSpec((d, d), lambda i: (0, 0)),
            pl.BlockSpec((1, d), lambda i: (0, 0)),
        ],
        out_specs=pl.BlockSpec((tm, d), lambda i: (i, 0)),
        out_shape=jax.ShapeDtypeStruct((t, d), F32),
        scratch_shapes=[pltpu.VMEM((tm, d), BF16)],
        compiler_params=_params("arbitrary"),
        name="gated_out_proj",
    )(a, z_arr, x2, gate, w_bf16, final_g.reshape(1, d))


def _cos_sin(n_rows, n_cols, period):
    r = jnp.arange(n_rows, dtype=jnp.int32)[:, None]
    c = jnp.arange(n_cols, dtype=jnp.int32)[None, :]
    ang = ((r * c) % period).astype(F32) * (2.0 * math.pi / period)
    return jnp.cos(ang), jnp.sin(ang)


def _pack_complex(re, im):
    hi = lax.bitcast_convert_type(re.astype(BF16).astype(F32), jnp.uint32)
    lo = lax.bitcast_convert_type(im.astype(BF16).astype(F32), jnp.uint32)
    return hi | (lo >> 16)


def _unpack_complex(w):
    re = lax.bitcast_convert_type(w & jnp.uint32(0xFFFF0000), F32)
    im = lax.bitcast_convert_type(w << 16, F32)
    return re.astype(BF16), im.astype(BF16)


def _chan_dft_kernel(u_ref, m_ref, z_ref, *, groups):
    gs = m_ref.shape[0]
    for g in range(groups):
        cols = slice(g * gs, (g + 1) * gs)
        r = jnp.dot(u_ref[:, cols], m_ref[...], preferred_element_type=F32)
        z_ref[:, cols] = _pack_complex(r[:, :gs], r[:, gs:])


def _chan_dft(proj, d):
    t = proj.shape[0]
    gs = d // FOURIER_GROUPS
    c, s = _cos_sin(gs, gs, gs)
    m = jnp.concatenate([c, -s], axis=1).astype(BF16)
    tm = min(1024, t)
    assert t % tm == 0
    return pl.pallas_call(
        functools.partial(_chan_dft_kernel, groups=FOURIER_GROUPS),
        grid=(t // tm,),
        in_specs=[
            pl.BlockSpec((tm, d), lambda i: (i, 0)),
            pl.BlockSpec((gs, 2 * gs), lambda i: (0, 0)),
        ],
        out_specs=pl.BlockSpec((tm, d), lambda i: (i, 0)),
        out_shape=jax.ShapeDtypeStruct((t, d), jnp.uint32),
        compiler_params=_params("arbitrary"),
        name="fourier_chan_dft",
    )(proj, m)


def _seq_dft1_kernel(z_ref, m_ref, twc_ref, tws_ref, y_ref):
    n1 = z_ref.shape[1]
    twc = twc_ref[0]
    tws = tws_ref[0]
    for q in range(z_ref.shape[2]):
        zr, zi = _unpack_complex(z_ref[0, :, q, :])
        y = jnp.dot(m_ref[...], jnp.concatenate([zr, zi], axis=0), preferred_element_type=F32)
        yr = y[:n1]
        yi = y[n1:]
        c = twc[:, q:q + 1]
        s = tws[:, q:q + 1]
        y_ref[0, :, q, :] = _pack_complex(yr * c + yi * s, yi * c - yr * s)


def _seq_dft1(z, b, seq, d):
    n1 = DFT_N1
    n2 = seq // n1
    nq = SUBLANES
    dt = min(1024, d)
    assert seq % n1 == 0 and n2 % nq == 0 and d % dt == 0
    nj = n2 // nq
    c1, s1 = _cos_sin(n1, n1, n1)
    m = jnp.concatenate([jnp.concatenate([c1, s1], axis=1),
                         jnp.concatenate([-s1, c1], axis=1)], axis=0).astype(BF16)
    twc, tws = _cos_sin(n1, n2, seq)
    twc = twc.reshape(n1, nj, nq).transpose(1, 0, 2)
    tws = tws.reshape(n1, nj, nq).transpose(1, 0, 2)
    blk = pl.BlockSpec((1, n1, nq, dt), lambda bi, j, k: (bi, 0, j, k))
    return pl.pallas_call(
        _seq_dft1_kernel,
        grid=(b, nj, d // dt),
        in_specs=[
            blk,
            pl.BlockSpec((2 * n1, 2 * n1), lambda bi, j, k: (0, 0)),
            pl.BlockSpec((1, n1, nq), lambda bi, j, k: (j, 0, 0)),
            pl.BlockSpec((1, n1, nq), lambda bi, j, k: (j, 0, 0)),
        ],
        out_specs=blk,
        out_shape=jax.ShapeDtypeStruct((b, n1, n2, d), jnp.uint32),
        compiler_params=_params("arbitrary", "arbitrary", "arbitrary"),
        name="fourier_seq_dft_stage1",
    )(z.reshape(b, n1, n2, d), m, twc, tws)


def _seq_dft2_kernel(y_ref, m_ref, o_ref, *, inv_norm):
    for q in range(y_ref.shape[1]):
        yr, yi = _unpack_complex(y_ref[0, q])
        r = jnp.dot(m_ref[...], jnp.concatenate([yr, yi], axis=0), preferred_element_type=F32)
        o_ref[0, :, q, :] = r * inv_norm


def _seq_dft2(y, inv_norm):
    b, n1, n2, d = y.shape
    nq = SUBLANES
    dt = min(1024, d)
    assert n1 % nq == 0 and d % dt == 0
    c2, s2 = _cos_sin(n2, n2, n2)
    m = jnp.concatenate([c2, s2], axis=1).astype(BF16)
    out = pl.pallas_call(
        functools.partial(_seq_dft2_kernel, inv_norm=inv_norm),
        grid=(b, n1 // nq, d // dt),
        in_specs=[
            pl.BlockSpec((1, nq, n2, dt), lambda bi, j, k: (bi, j, 0, k)),
            pl.BlockSpec((n2, 2 * n2), lambda bi, j, k: (0, 0)),
        ],
        out_specs=pl.BlockSpec((1, n2, nq, dt), lambda bi, j, k: (bi, 0, j, k)),
        out_shape=jax.ShapeDtypeStruct((b, n2, n1, d), F32),
        compiler_params=_params("arbitrary", "arbitrary", "arbitrary"),
        name="fourier_seq_dft_stage2",
    )(y, m)
    return out.reshape(b * n2 * n1, d)


def _seq_dft_direct_kernel(z_ref, m_ref, o_ref, *, inv_norm):
    zr, zi = _unpack_complex(z_ref[0])
    o_ref[0] = jnp.dot(m_ref[...], jnp.concatenate([zr, zi], axis=0), preferred_element_type=F32) * inv_norm


def _seq_dft_direct(z, b, seq, d, inv_norm):
    dt = min(1024, d)
    c, s = _cos_sin(seq, seq, seq)
    m = jnp.concatenate([c, s], axis=1).astype(BF16)
    out = pl.pallas_call(
        functools.partial(_seq_dft_direct_kernel, inv_norm=inv_norm),
        grid=(b, d // dt),
        in_specs=[
            pl.BlockSpec((1, seq, dt), lambda bi, k: (bi, 0, k)),
            pl.BlockSpec((seq, 2 * seq), lambda bi, k: (0, 0)),
        ],
        out_specs=pl.BlockSpec((1, seq, dt), lambda bi, k: (bi, 0, k)),
        out_shape=jax.ShapeDtypeStruct((b, seq, d), F32),
        compiler_params=_params("arbitrary", "arbitrary"),
        name="fourier_seq_dft_direct",
    )(z.reshape(b, seq, d), m)
    return out.reshape(b * seq, d)


def _fourier_mixed(proj, b, seq, d, two_stage):
    z = _chan_dft(proj, d)
    inv_norm = 1.0 / math.sqrt(seq * (d // FOURIER_GROUPS))
    if two_stage:
        return _seq_dft2(_seq_dft1(z, b, seq, d), inv_norm)
    return _seq_dft_direct(z, b, seq, d, inv_norm)


def _rope_tables(n_tokens):
    rows = n_tokens // GRID_W
    row = jnp.repeat(jnp.arange(rows), GRID_W).astype(F32)
    col = jnp.tile(jnp.arange(GRID_W), rows).astype(F32)
    quarter = HEAD_DIM // 4
    inv_freq = ROPE_BASE ** (-jnp.arange(quarter, dtype=F32) / quarter)
    ang_r = row[:, None] * inv_freq[None, :]
    ang_c = col[:, None] * inv_freq[None, :]
    ang = jnp.concatenate([ang_r, ang_r, ang_c, ang_c], axis=-1)
    sign = jnp.where((jnp.arange(HEAD_DIM) % (HEAD_DIM // 2)) < quarter, -1.0, 1.0).astype(F32)
    reps = LANES // HEAD_DIM
    return jnp.tile(jnp.cos(ang), (1, reps)), jnp.tile(jnp.sin(ang) * sign, (1, reps))


def _rope128(xf, cos, sin_signed):
    quarter = HEAD_DIM // 4
    lane = lax.broadcasted_iota(jnp.int32, xf.shape, 1)
    from_hi = pltpu.roll(xf, LANES - quarter, axis=1)
    from_lo = pltpu.roll(xf, quarter, axis=1)
    partner = jnp.where((lane % (HEAD_DIM // 2)) < quarter, from_hi, from_lo)
    return xf * cos + partner * sin_signed


def _k_rope_kernel(k_ref, cos_ref, sin_ref, o_ref):
    cos = cos_ref[...]
    sin = sin_ref[...]
    for c in range(k_ref.shape[1] // LANES):
        cols = slice(c * LANES, (c + 1) * LANES)
        o_ref[:, cols] = _rope128(k_ref[:, cols].astype(F32), cos, sin).astype(BF16)


def _k_rope(proj, k_col, kvw, seq, cos, sin):
    t = proj.shape[0]
    tm = min(1024, seq)
    assert seq % tm == 0 and kvw % LANES == 0
    tiles_per_seq = seq // tm
    return pl.pallas_call(
        _k_rope_kernel,
        grid=(t // tm,),
        in_specs=[
            pl.BlockSpec((tm, kvw), lambda i: (i, k_col)),
            pl.BlockSpec((tm, LANES), lambda i: (i % tiles_per_seq, 0)),
            pl.BlockSpec((tm, LANES), lambda i: (i % tiles_per_seq, 0)),
        ],
        out_specs=pl.BlockSpec((tm, kvw), lambda i: (i, 0)),
        out_shape=jax.ShapeDtypeStruct((t, kvw), BF16),
        compiler_params=_params("arbitrary"),
        name="attn_k_rope",
    )(proj, cos, sin)


def _head_halves(cat_f32, lo):
    lane = lax.broadcasted_iota(jnp.int32, cat_f32.shape, 1)
    same = jnp.where((lane >= lo) & (lane < lo + HEAD_DIM), cat_f32, 0.0)
    swap = pltpu.roll(same, HEAD_DIM, axis=1)
    low, high = (same, swap) if lo == 0 else (swap, same)
    return low.astype(BF16), high.astype(BF16)


def _attn_kernel(*refs, band, bq, seq, n_ctx, groups):
    if band:
        (sink_ref, q_ref, cos_ref, sin_ref, kp_ref, kc_ref, kn_ref,
         vp_ref, vc_ref, vn_ref, kx_ref, vx_ref, o_ref) = refs
    else:
        sink_ref, q_ref, kx_ref, vx_ref, o_ref = refs
    n_band = bq + 2 * WINDOW if band else 0
    nk = n_band + n_ctx
    if band:
        start = pl.program_id(1) * bq
        qpos = start + lax.broadcasted_iota(jnp.int32, (bq, nk), 0)
        col = lax.broadcasted_iota(jnp.int32, (bq, nk), 1)
        kpos = start - WINDOW + col
        valid = ((jnp.abs(qpos - kpos) <= WINDOW) & (kpos >= 0) & (kpos < seq)) | (col >= n_band)
        cos = cos_ref[...]
        sin = sin_ref[...]
    lane_q = lax.broadcasted_iota(jnp.int32, (bq, LANES), 1)

    def probs(s, sink):
        if band:
            s = jnp.where(valid, s, NEG_INF)
        m = jnp.maximum(jnp.max(s, axis=-1, keepdims=True), sink)
        p = jnp.exp(s - m)
        denom = jnp.sum(p, axis=-1, keepdims=True) + jnp.exp(sink - m)
        return p.astype(BF16), 1.0 / denom

    for h in range(N_KV_HEADS):
        pair_cols = slice((h // 2) * LANES, (h // 2 + 1) * LANES)
        if band:
            k_parts = [kp_ref[:, pair_cols], kc_ref[:, pair_cols], kn_ref[:, pair_cols], kx_ref[:, pair_cols]]
            v_parts = [vp_ref[:, pair_cols], vc_ref[:, pair_cols], vn_ref[:, pair_cols], vx_ref[:, pair_cols]]
            kcat = jnp.concatenate([p.astype(F32) for p in k_parts], axis=0)
            vcat = jnp.concatenate([p.astype(F32) for p in v_parts], axis=0)
        else:
            kcat = kx_ref[:, pair_cols].astype(F32)
            vcat = vx_ref[:, pair_cols].astype(F32)
        lo = (h % 2) * HEAD_DIM
        k_lo, k_hi = _head_halves(kcat, lo)
        v_lo, v_hi = _head_halves(vcat, lo)
        for p in range(groups // 2):
            off = (h * groups + 2 * p) * HEAD_DIM
            cols = slice(off, off + LANES)
            qf = q_ref[:, cols].astype(F32)
            if band:
                qf = _rope128(qf, cos, sin)
            qb = (qf * ATTN_SCALE).astype(BF16)
            nt = (((1,), (1,)), ((), ()))
            s0 = lax.dot_general(qb, k_lo, nt, preferred_element_type=F32)
            s1 = lax.dot_general(qb, k_hi, nt, preferred_element_type=F32)
            p0, inv0 = probs(s0, sink_ref[h * groups + 2 * p])
            p1, inv1 = probs(s1, sink_ref[h * groups + 2 * p + 1])
            o_pair = (jnp.dot(p0, v_lo, preferred_element_type=F32)
                      + jnp.dot(p1, v_hi, preferred_element_type=F32))
            o_ref[:, cols] = (o_pair * jnp.where(lane_q < HEAD_DIM, inv0, inv1)).astype(BF16)


def _attention(sink, proj, q_col, v_col, k_rot, proj_c, kc_col, vc_col, b, seq, n_ctx, d, cos, sin):
    kvw = N_KV_HEADS * HEAD_DIM
    groups = d // kvw
    bq = min(256, seq)
    assert seq % bq == 0 and bq % WINDOW == 0 and groups % 2 == 0 and kvw % LANES == 0
    nq = seq // bq
    wpb = bq // WINDOW
    n_halo = b * seq // WINDOW

    def prev_map(col):
        return lambda bi, i: (jnp.maximum((bi * nq + i) * wpb - 1, 0), col)

    def next_map(col):
        return lambda bi, i: (jnp.minimum((bi * nq + i + 1) * wpb, n_halo - 1), col)

    def cur_map(col):
        return lambda bi, i: (bi * nq + i, col)

    return pl.pallas_call(
        functools.partial(_attn_kernel, band=True, bq=bq, seq=seq, n_ctx=n_ctx, groups=groups),
        grid=(b, nq),
        in_specs=[
            pl.BlockSpec(memory_space=pltpu.SMEM),
            pl.BlockSpec((bq, d), cur_map(q_col)),
            pl.BlockSpec((bq, LANES), lambda bi, i: (i, 0)),
            pl.BlockSpec((bq, LANES), lambda bi, i: (i, 0)),
            pl.BlockSpec((WINDOW, kvw), prev_map(0)),
            pl.BlockSpec((bq, kvw), cur_map(0)),
            pl.BlockSpec((WINDOW, kvw), next_map(0)),
            pl.BlockSpec((WINDOW, kvw), prev_map(v_col)),
            pl.BlockSpec((bq, kvw), cur_map(v_col)),
            pl.BlockSpec((WINDOW, kvw), next_map(v_col)),
            pl.BlockSpec((n_ctx, kvw), lambda bi, i: (bi, kc_col)),
            pl.BlockSpec((n_ctx, kvw), lambda bi, i: (bi, vc_col)),
        ],
        out_specs=pl.BlockSpec((bq, d), lambda bi, i: (bi * nq + i, 0)),
        out_shape=jax.ShapeDtypeStruct((b * seq, d), BF16),
        compiler_params=_params("arbitrary", "arbitrary"),
        name="window_attention",
    )(sink, proj, cos, sin, k_rot, k_rot, k_rot, proj, proj, proj, proj_c, proj_c)


def _ctx_attention(sink, proj_c, q_col, k_col, v_col, b, n_ctx, d):
    kvw = N_KV_HEADS * HEAD_DIM
    groups = d // kvw
    return pl.pallas_call(
        functools.partial(_attn_kernel, band=False, bq=n_ctx, seq=n_ctx, n_ctx=n_ctx, groups=groups),
        grid=(b,),
        in_specs=[
            pl.BlockSpec(memory_space=pltpu.SMEM),
            pl.BlockSpec((n_ctx, d), lambda bi: (bi, q_col)),
            pl.BlockSpec((n_ctx, kvw), lambda bi: (bi, k_col)),
            pl.BlockSpec((n_ctx, kvw), lambda bi: (bi, v_col)),
        ],
        out_specs=pl.BlockSpec((n_ctx, d), lambda bi: (bi, 0)),
        out_shape=jax.ShapeDtypeStruct((b * n_ctx, d), BF16),
        compiler_params=_params("arbitrary"),
        name="context_attention",
    )(sink, proj_c, proj_c, proj_c)


def _conv_kernel(a_ref, ag_ref, ap_ref, agp_ref, an_ref, agn_ref, w_ref, b_ref, lg_ref, lb_ref,
                 o_ref, g_scr, sh_scr, c_scr, *, tiles_per_seq, kw):
    tm, d = a_ref.shape
    halo = CONV_HALO
    span = tm + 2 * halo
    i = pl.program_id(0)
    first = (i % tiles_per_seq) == 0
    last = (i % tiles_per_seq) == tiles_per_seq - 1

    def glu(a, ag):
        return a.astype(F32) * jax.nn.sigmoid(ag.astype(F32))

    g_scr[0:halo, :] = jnp.where(first, 0.0, glu(ap_ref[...], agp_ref[...]))

    def glu_body(r, carry):
        rows = pl.ds(pl.multiple_of(r * ROW_CHUNK, ROW_CHUNK), ROW_CHUNK)
        g_scr[pl.ds(pl.multiple_of(halo + r * ROW_CHUNK, ROW_CHUNK), ROW_CHUNK), :] = glu(a_ref[rows, :], ag_ref[rows, :])
        return carry

    lax.fori_loop(0, tm // ROW_CHUNK, glu_body, 0, unroll=ROW_UNROLL)
    g_scr[halo + tm:span, :] = jnp.where(last, 0.0, glu(an_ref[...], agn_ref[...]))
    g_scr[span:span + SUBLANES, :] = jnp.zeros((SUBLANES, d), F32)

    for r in range(SUBLANES):
        sh_scr[r] = g_scr[r:r + span, :]

    pad = kw // 2
    lw = 4 * LANES
    bias = b_ref[...]
    lg = lg_ref[...]
    lb = lb_ref[...]

    def row_body(rc, carry):
        base = pl.multiple_of(rc * ROW_CHUNK, ROW_CHUNK)
        for ch in range(d // lw):
            cols = slice(ch * lw, (ch + 1) * lw)
            acc = [jnp.zeros((ROW_CHUNK, lw), F32) for _ in range(CONV_ACCS)]
            for j in range(kw):
                off = j + halo - pad
                rows = pl.ds(pl.multiple_of(base + (off // SUBLANES) * SUBLANES, SUBLANES), ROW_CHUNK)
                acc[j % CONV_ACCS] = acc[j % CONV_ACCS] + w_ref[j:j + 1, cols] * sh_scr[off % SUBLANES, rows, cols]
            c_scr[pl.ds(base, ROW_CHUNK), cols] = sum(acc[1:], acc[0]) + bias[:, cols]
        y = c_scr[pl.ds(base, ROW_CHUNK), :]
        mu = jnp.mean(y, axis=-1, keepdims=True)
        var = jnp.mean(jnp.square(y - mu), axis=-1, keepdims=True)
        yn = (y - mu) * lax.rsqrt(var + NORM_EPS) * lg + lb
        o_ref[pl.ds(base, ROW_CHUNK), :] = _silu(yn).astype(BF16)
        return carry

    lax.fori_loop(0, tm // ROW_CHUNK, row_body, 0)


def _conv_module(proj, seq, d, dw_w, dw_b, ln_g, ln_b):
    t = proj.shape[0]
    kw = dw_w.shape[0]
    tm = min(256, seq)
    halo = CONV_HALO
    assert seq % tm == 0 and tm % halo == 0 and kw // 2 <= halo and d % (4 * LANES) == 0
    tiles_per_seq = seq // tm
    hpt = tm // halo
    n_halo = t // halo
    span = tm + 2 * halo

    def prev_map(col):
        return lambda i: (jnp.maximum(i * hpt - 1, 0), col)

    def next_map(col):
        return lambda i: (jnp.minimum((i + 1) * hpt, n_halo - 1), col)

    vec = lambda: pl.BlockSpec((1, d), lambda i: (0, 0))
    return pl.pallas_call(
        functools.partial(_conv_kernel, tiles_per_seq=tiles_per_seq, kw=kw),
        grid=(t // tm,),
        in_specs=[
            pl.BlockSpec((tm, d), lambda i: (i, 0)),
            pl.BlockSpec((tm, d), lambda i: (i, 1)),
            pl.BlockSpec((halo, d), prev_map(0)),
            pl.BlockSpec((halo, d), prev_map(1)),
            pl.BlockSpec((halo, d), next_map(0)),
            pl.BlockSpec((halo, d), next_map(1)),
            pl.BlockSpec((kw, d), lambda i: (0, 0)),
            vec(), vec(), vec(),
        ],
        out_specs=pl.BlockSpec((tm, d), lambda i: (i, 0)),
        out_shape=jax.ShapeDtypeStruct((t, d), BF16),
        scratch_shapes=[
            pltpu.VMEM((span + SUBLANES, d), F32),
            pltpu.VMEM((SUBLANES, span, d), F32),
            pltpu.VMEM((tm, d), F32),
        ],
        compiler_params=_params("arbitrary"),
        name="conformer_conv",
    )(proj, proj, proj, proj, proj, proj, dw_w, dw_b.reshape(1, d), ln_g.reshape(1, d), ln_b.reshape(1, d))


def kernel(x, c, ctx, c_ctx, norm_g, ada_w, ada_b, four_w_in, four_w_out, attn_w_in, attn_sink,
           attn_w_out, conv_w_in, conv_dw_w, conv_dw_b, conv_ln_g, conv_ln_b, conv_w_out, final_g):
    b, seq, d = x.shape
    n_ctx = ctx.shape[1]
    depth = norm_g.shape[0]
    kvw = N_KV_HEADS * HEAD_DIM
    assert d % kvw == 0 and seq % GRID_W == 0

    n_rows = -(-(b + 1) // SUBLANES) * SUBLANES
    cond = jnp.concatenate([c, c_ctx[None, :], jnp.zeros((n_rows - b - 1, d), F32)], axis=0)
    mod = _modulation(cond, ada_w, ada_b)

    cos, sin = _rope_tables(seq)
    qe = d
    ke = qe + kvw
    ve = ke + kvw
    attn_w_perm = jnp.concatenate([attn_w_in[:, :, ve:], attn_w_in[:, :, :ve]], axis=-1)

    x2 = x.reshape(b * seq, d)
    c2 = ctx.reshape(b * n_ctx, d)
    for i in range(depth):
        kind, j = i % N_MIXERS, i // N_MIXERS
        with_ctx = i < depth - 1
        last = i == depth - 1
        shift = mod[i, :, 0:d].reshape(n_rows, 1, d)
        scale = mod[i, :, d:2 * d].reshape(n_rows, 1, d)
        gate = mod[i, :, 2 * d:3 * d].reshape(n_rows, 1, d)
        if kind == 0:
            w_in = four_w_in[j].astype(BF16)
            w_out = four_w_out[j].astype(BF16)
            proj = _in_proj(x2, norm_g[i], shift, scale, w_in, seq, 0)
            a = _fourier_mixed(proj, b, seq, d, two_stage=True)
            z_col = 1
            if with_ctx:
                proj_c = _in_proj(c2, norm_g[i], shift, scale, w_in, b * n_ctx, b)
                a_c = _fourier_mixed(proj_c, b, n_ctx, d, two_stage=False)
        elif kind == 1:
            w_in = attn_w_perm[j].astype(BF16)
            w_out = attn_w_out[j].astype(BF16)
            z_col, q_col = 0, 1
            k_col = 2 * d // kvw
            v_col = k_col + 1
            proj = _in_proj(x2, norm_g[i], shift, scale, w_in, seq, 0)
            proj_c = _in_proj(c2, norm_g[i], shift, scale, w_in, b * n_ctx, b)
            k_rot = _k_rope(proj, k_col, kvw, seq, cos, sin)
            a = _attention(attn_sink[j], proj, q_col, v_col, k_rot, proj_c, k_col, v_col,
                           b, seq, n_ctx, d, cos, sin)
            if with_ctx:
                a_c = _ctx_attention(attn_sink[j], proj_c, q_col, k_col, v_col, b, n_ctx, d)
        else:
            w_in = conv_w_in[j].astype(BF16)
            w_out = conv_w_out[j].astype(BF16)
            z_col = 2
            conv_args = (conv_dw_w[j], conv_dw_b[j], conv_ln_g[j], conv_ln_b[j])
            proj = _in_proj(x2, norm_g[i], shift, scale, w_in, seq, 0)
            a = _conv_module(proj, seq, d, *conv_args)
            if with_ctx:
                proj_c = _in_proj(c2, norm_g[i], shift, scale, w_in, b * n_ctx, b)
                a_c = _conv_module(proj_c, n_ctx, d, *conv_args)
        x2 = _out_proj(a, proj, z_col, x2, gate, w_out, final_g, seq, 0, final_norm=last)
        if with_ctx:
            c2 = _out_proj(a_c, proj_c, z_col, c2, gate, w_out, final_g, b * n_ctx, b, final_norm=False)
    return x2.reshape(b, seq, d)
```

```python
import functools
import math

import jax
import jax.numpy as jnp
from jax import lax
from jax.experimental import pallas as pl
from jax.experimental.pallas import tpu as pltpu

HEAD_DIM = 64
N_KV_HEADS = 4
FOURIER_GROUPS = 8
WINDOW = 128
GRID_W = 64
N_MIXERS = 3
NORM_EPS = 1e-6
ROPE_BASE = 10000.0
NEG_INF = -1e30
ATTN_SCALE = HEAD_DIM ** -0.5
LOG2_E = math.log2(math.e)

LANES = 128
SUBLANES = 8
VMEM_LIMIT_BYTES = 56 * 1024 * 1024

DFT_N1 = 128
ROW_CHUNK = 16
ROW_UNROLL = 4
CONV_HALO = 16

F32 = jnp.float32
BF16 = jnp.bfloat16


def _params(*sem):
    return pltpu.CompilerParams(dimension_semantics=sem, vmem_limit_bytes=VMEM_LIMIT_BYTES)


def _silu(v):
    return v * jax.nn.sigmoid(v)


def _pick_tile(n, cap, quantum):
    if n <= cap:
        return n
    t = cap - cap % quantum
    while n % t:
        t -= quantum
    return t


def _mod_kernel(cond_ref, w_ref, b_ref, o_ref):
    s = _silu(cond_ref[...]).astype(BF16)
    o_ref[0] = jnp.dot(s, w_ref[0].astype(BF16), preferred_element_type=F32) + b_ref[0]


def _modulation(cond, ada_w, ada_b):
    depth, d, d3 = ada_w.shape
    r = cond.shape[0]
    tn = _pick_tile(d3, 1024, 2 * LANES)
    return pl.pallas_call(
        _mod_kernel,
        grid=(depth, d3 // tn),
        in_specs=[
            pl.BlockSpec((r, d), lambda l, j: (0, 0)),
            pl.BlockSpec((1, d, tn), lambda l, j: (l, 0, j)),
            pl.BlockSpec((1, 1, tn), lambda l, j: (l, 0, j)),
        ],
        out_specs=pl.BlockSpec((1, r, tn), lambda l, j: (l, 0, j)),
        out_shape=jax.ShapeDtypeStruct((depth, r, d3), F32),
        compiler_params=_params("arbitrary", "arbitrary"),
        name="ada_modulation",
    )(cond, ada_w, ada_b.reshape(depth, 1, d3))


def _in_proj_kernel(x_ref, g_ref, shift_ref, scale_ref, w_ref, o_ref, h_scr):
    tm = x_ref.shape[0]

    @pl.when(pl.program_id(1) == 0)
    def _():
        g = g_ref[...]
        sh = shift_ref[0]
        sc = 1.0 + scale_ref[0]

        def body(r, carry):
            rows = pl.ds(pl.multiple_of(r * ROW_CHUNK, ROW_CHUNK), ROW_CHUNK)
            xf = x_ref[rows, :]
            ms = jnp.mean(xf * xf, axis=-1, keepdims=True)
            y = xf * lax.rsqrt(ms + NORM_EPS)
            h_scr[rows, :] = ((y * g) * sc + sh).astype(BF16)
            return carry

        lax.fori_loop(0, tm // ROW_CHUNK, body, 0, unroll=ROW_UNROLL)

    o_ref[...] = jnp.dot(h_scr[...], w_ref[...], preferred_element_type=F32).astype(BF16)


def _in_proj(x2, norm_g, shift, scale, w_bf16, rows_per_mod, mod_row0):
    t, d = x2.shape
    n_out = w_bf16.shape[1]
    tm = min(1024, rows_per_mod, t)
    tn = _pick_tile(n_out, 1024, 2 * LANES)
    assert t % tm == 0 and rows_per_mod % tm == 0 and n_out % tn == 0
    tiles_per_mod = rows_per_mod // tm
    mod_map = lambda i, j: (mod_row0 + i // tiles_per_mod, 0, 0)
    return pl.pallas_call(
        _in_proj_kernel,
        grid=(t // tm, n_out // tn),
        in_specs=[
            pl.BlockSpec((tm, d), lambda i, j: (i, 0)),
            pl.BlockSpec((1, d), lambda i, j: (0, 0)),
            pl.BlockSpec((1, 1, d), mod_map),
            pl.BlockSpec((1, 1, d), mod_map),
            pl.BlockSpec((d, tn), lambda i, j: (0, j)),
        ],
        out_specs=pl.BlockSpec((tm, tn), lambda i, j: (i, j)),
        out_shape=jax.ShapeDtypeStruct((t, n_out), BF16),
        scratch_shapes=[pltpu.VMEM((tm, d), BF16)],
        compiler_params=_params("arbitrary", "arbitrary"),
        name="norm_in_proj",
    )(x2, norm_g.reshape(1, d), shift, scale, w_bf16)


def _out_proj_kernel(a_ref, z_ref, x_ref, gate_ref, w_ref, fg_ref, o_ref, y_scr, *, final_norm):
    tm = x_ref.shape[0]
    n_chunks = tm // ROW_CHUNK

    def gate_body(r, carry):
        rows = pl.ds(pl.multiple_of(r * ROW_CHUNK, ROW_CHUNK), ROW_CHUNK)
        y_scr[rows, :] = (a_ref[rows, :].astype(F32) * _silu(z_ref[rows, :].astype(F32))).astype(BF16)
        return carry

    lax.fori_loop(0, n_chunks, gate_body, 0, unroll=ROW_UNROLL)
    o_ref[...] = jnp.dot(y_scr[...], w_ref[...], preferred_element_type=F32)
    gate = gate_ref[0]
    fg = fg_ref[...]

    def res_body(r, carry):
        rows = pl.ds(pl.multiple_of(r * ROW_CHUNK, ROW_CHUNK), ROW_CHUNK)
        xn = x_ref[rows, :] + gate * o_ref[rows, :]
        if final_norm:
            ms = jnp.mean(xn * xn, axis=-1, keepdims=True)
            xn = (xn * lax.rsqrt(ms + NORM_EPS)) * fg
        o_ref[rows, :] = xn
        return carry

    lax.fori_loop(0, n_chunks, res_body, 0, unroll=ROW_UNROLL)


def _out_proj(a, z_arr, z_col, x2, gate, w_bf16, final_g, rows_per_mod, mod_row0, final_norm):
    t, d = x2.shape
    tm = min(512, rows_per_mod, t)
    assert t % tm == 0 and rows_per_mod % tm == 0
    tiles_per_mod = rows_per_mod // tm
    return pl.pallas_call(
        functools.partial(_out_proj_kernel, final_norm=final_norm),
        grid=(t // tm,),
        in_specs=[
            pl.BlockSpec((tm, d), lambda i: (i, 0)),
            pl.BlockSpec((tm, d), lambda i: (i, z_col)),
            pl.BlockSpec((tm, d), lambda i: (i, 0)),
            pl.BlockSpec((1, 1, d), lambda i: (mod_row0 + i // tiles_per_mod, 0, 0)),
            pl.BlockSpec((d, d), lambda i: (0, 0)),
            pl.BlockSpec((1, d), lambda i: (0, 0)),
        ],
        out_specs=pl.BlockSpec((tm, d), lambda i: (i, 0)),
        out_shape=jax.ShapeDtypeStruct((t, d), F32),
        scratch_shapes=[pltpu.VMEM((tm, d), BF16)],
        compiler_params=_params("arbitrary"),
        name="gated_out_proj",
    )(a, z_arr, x2, gate, w_bf16, final_g.reshape(1, d))


def _cos_sin(n_rows, n_cols, period):
    r = jnp.arange(n_rows, dtype=jnp.int32)[:, None]
    c = jnp.arange(n_cols, dtype=jnp.int32)[None, :]
    ang = ((r * c) % period).astype(F32) * (2.0 * math.pi / period)
    return jnp.cos(ang), jnp.sin(ang)


def _pack_complex(re, im):
    hi = lax.bitcast_convert_type(re.astype(BF16).astype(F32), jnp.uint32)
    lo = lax.bitcast_convert_type(im.astype(BF16).astype(F32), jnp.uint32)
    return hi | (lo >> 16)


def _unpack_complex(w):
    re = lax.bitcast_convert_type(w & jnp.uint32(0xFFFF0000), F32)
    im = lax.bitcast_convert_type(w << 16, F32)
    return re.astype(BF16), im.astype(BF16)


def _chan_dft_kernel(u_ref, m_ref, z_ref, *, groups):
    gs = m_ref.shape[0]
    for g in range(groups):
        cols = slice(g * gs, (g + 1) * gs)
        r = jnp.dot(u_ref[:, cols], m_ref[...], preferred_element_type=F32)
        z_ref[:, cols] = _pack_complex(r[:, :gs], r[:, gs:])


def _chan_dft(proj, d):
    t = proj.shape[0]
    gs = d // FOURIER_GROUPS
    c, s = _cos_sin(gs, gs, gs)
    m = jnp.concatenate([c, -s], axis=1).astype(BF16)
    tm = min(1024, t)
    assert t % tm == 0
    return pl.pallas_call(
        functools.partial(_chan_dft_kernel, groups=FOURIER_GROUPS),
        grid=(t // tm,),
        in_specs=[
            pl.BlockSpec((tm, d), lambda i: (i, 0)),
            pl.BlockSpec((gs, 2 * gs), lambda i: (0, 0)),
        ],
        out_specs=pl.BlockSpec((tm, d), lambda i: (i, 0)),
        out_shape=jax.ShapeDtypeStruct((t, d), jnp.uint32),
        compiler_params=_params("arbitrary"),
        name="fourier_chan_dft",
    )(proj, m)


def _seq_dft1_kernel(z_ref, m_ref, twc_ref, tws_ref, y_ref, zg_scr, yg_scr):
    n1 = z_ref.shape[1]
    nq = z_ref.shape[2]
    twc = twc_ref[0]
    tws = tws_ref[0]
    for q in range(nq):
        zg_scr[q] = z_ref[0, :, q, :]
    for q in range(nq):
        zr, zi = _unpack_complex(zg_scr[q])
        y = jnp.dot(m_ref[...], jnp.concatenate([zr, zi], axis=0), preferred_element_type=F32)
        yr = y[:n1]
        yi = y[n1:]
        c = twc[:, q:q + 1]
        s = tws[:, q:q + 1]
        yg_scr[q] = _pack_complex(yr * c + yi * s, yi * c - yr * s)
    for q in range(nq):
        y_ref[0, :, q, :] = yg_scr[q]


def _seq_dft1(z, b, seq, d):
    n1 = DFT_N1
    n2 = seq // n1
    nq = SUBLANES
    dt = min(1024, d)
    assert seq % n1 == 0 and n2 % nq == 0 and d % dt == 0
    nj = n2 // nq
    c1, s1 = _cos_sin(n1, n1, n1)
    m = jnp.concatenate([jnp.concatenate([c1, s1], axis=1),
                         jnp.concatenate([-s1, c1], axis=1)], axis=0).astype(BF16)
    twc, tws = _cos_sin(n1, n2, seq)
    twc = twc.reshape(n1, nj, nq).transpose(1, 0, 2)
    tws = tws.reshape(n1, nj, nq).transpose(1, 0, 2)
    blk = pl.BlockSpec((1, n1, nq, dt), lambda bi, j, k: (bi, 0, j, k))
    return pl.pallas_call(
        _seq_dft1_kernel,
        grid=(b, nj, d // dt),
        in_specs=[
            blk,
            pl.BlockSpec((2 * n1, 2 * n1), lambda bi, j, k: (0, 0)),
            pl.BlockSpec((1, n1, nq), lambda bi, j, k: (j, 0, 0)),
            pl.BlockSpec((1, n1, nq), lambda bi, j, k: (j, 0, 0)),
        ],
        out_specs=blk,
        out_shape=jax.ShapeDtypeStruct((b, n1, n2, d), jnp.uint32),
        scratch_shapes=[pltpu.VMEM((nq, n1, dt), jnp.uint32), pltpu.VMEM((nq, n1, dt), jnp.uint32)],
        compiler_params=_params("arbitrary", "arbitrary", "arbitrary"),
        name="fourier_seq_dft_stage1",
    )(z.reshape(b, n1, n2, d), m, twc, tws)


def _seq_dft2_kernel(y_ref, m_ref, o_ref, *, inv_norm):
    for q in range(y_ref.shape[1]):
        yr, yi = _unpack_complex(y_ref[0, q])
        r = jnp.dot(m_ref[...], jnp.concatenate([yr, yi], axis=0), preferred_element_type=F32)
        o_ref[0, :, q, :] = r * inv_norm


def _seq_dft2(y, inv_norm):
    b, n1, n2, d = y.shape
    nq = SUBLANES
    dt = min(1024, d)
    assert n1 % nq == 0 and d % dt == 0
    c2, s2 = _cos_sin(n2, n2, n2)
    m = jnp.concatenate([c2, s2], axis=1).astype(BF16)
    out = pl.pallas_call(
        functools.partial(_seq_dft2_kernel, inv_norm=inv_norm),
        grid=(b, n1 // nq, d // dt),
        in_specs=[
            pl.BlockSpec((1, nq, n2, dt), lambda bi, j, k: (bi, j, 0, k)),
            pl.BlockSpec((n2, 2 * n2), lambda bi, j, k: (0, 0)),
        ],
        out_specs=pl.BlockSpec((1, n2, nq, dt), lambda bi, j, k: (bi, 0, j, k)),
        out_shape=jax.ShapeDtypeStruct((b, n2, n1, d), F32),
        compiler_params=_params("arbitrary", "arbitrary", "arbitrary"),
        name="fourier_seq_dft_stage2",
    )(y, m)
    return out.reshape(b * n2 * n1, d)


def _seq_dft_direct_kernel(z_ref, m_ref, o_ref, *, inv_norm):
    zr, zi = _unpack_complex(z_ref[0])
    o_ref[0] = jnp.dot(m_ref[...], jnp.concatenate([zr, zi], axis=0), preferred_element_type=F32) * inv_norm


def _seq_dft_direct(z, b, seq, d, inv_norm):
    dt = min(1024, d)
    c, s = _cos_sin(seq, seq, seq)
    m = jnp.concatenate([c, s], axis=1).astype(BF16)
    out = pl.pallas_call(
        functools.partial(_seq_dft_direct_kernel, inv_norm=inv_norm),
        grid=(b, d // dt),
        in_specs=[
            pl.BlockSpec((1, seq, dt), lambda bi, k: (bi, 0, k)),
            pl.BlockSpec((seq, 2 * seq), lambda bi, k: (0, 0)),
        ],
        out_specs=pl.BlockSpec((1, seq, dt), lambda bi, k: (bi, 0, k)),
        out_shape=jax.ShapeDtypeStruct((b, seq, d), F32),
        compiler_params=_params("arbitrary", "arbitrary"),
        name="fourier_seq_dft_direct",
    )(z.reshape(b, seq, d), m)
    return out.reshape(b * seq, d)


def _fourier_mixed(proj, b, seq, d, two_stage):
    z = _chan_dft(proj, d)
    inv_norm = 1.0 / math.sqrt(seq * (d // FOURIER_GROUPS))
    if two_stage:
        return _seq_dft2(_seq_dft1(z, b, seq, d), inv_norm)
    return _seq_dft_direct(z, b, seq, d, inv_norm)


def _rope_tables(n_tokens):
    rows = n_tokens // GRID_W
    row = jnp.repeat(jnp.arange(rows), GRID_W).astype(F32)
    col = jnp.tile(jnp.arange(GRID_W), rows).astype(F32)
    quarter = HEAD_DIM // 4
    inv_freq = ROPE_BASE ** (-jnp.arange(quarter, dtype=F32) / quarter)
    ang_r = row[:, None] * inv_freq[None, :]
    ang_c = col[:, None] * inv_freq[None, :]
    ang = jnp.concatenate([ang_r, ang_r, ang_c, ang_c], axis=-1)
    sign = jnp.where((jnp.arange(HEAD_DIM) % (HEAD_DIM // 2)) < quarter, -1.0, 1.0).astype(F32)
    reps = LANES // HEAD_DIM
    return jnp.tile(jnp.cos(ang), (1, reps)), jnp.tile(jnp.sin(ang) * sign, (1, reps))


def _rope128(xf, cos, sin_signed):
    quarter = HEAD_DIM // 4
    lane = lax.broadcasted_iota(jnp.int32, xf.shape, 1)
    from_hi = pltpu.roll(xf, LANES - quarter, axis=1)
    from_lo = pltpu.roll(xf, quarter, axis=1)
    partner = jnp.where((lane % (HEAD_DIM // 2)) < quarter, from_hi, from_lo)
    return xf * cos + partner * sin_signed


def _k_rope_kernel(k_ref, cos_ref, sin_ref, o_ref):
    cos = cos_ref[...]
    sin = sin_ref[...]
    for c in range(k_ref.shape[1] // LANES):
        cols = slice(c * LANES, (c + 1) * LANES)
        o_ref[:, cols] = _rope128(k_ref[:, cols].astype(F32), cos, sin).astype(BF16)


def _k_rope(proj, k_col, kvw, seq, cos, sin):
    t = proj.shape[0]
    tm = min(1024, seq)
    assert seq % tm == 0 and kvw % LANES == 0
    tiles_per_seq = seq // tm
    return pl.pallas_call(
        _k_rope_kernel,
        grid=(t // tm,),
        in_specs=[
            pl.BlockSpec((tm, kvw), lambda i: (i, k_col)),
            pl.BlockSpec((tm, LANES), lambda i: (i % tiles_per_seq, 0)),
            pl.BlockSpec((tm, LANES), lambda i: (i % tiles_per_seq, 0)),
        ],
        out_specs=pl.BlockSpec((tm, kvw), lambda i: (i, 0)),
        out_shape=jax.ShapeDtypeStruct((t, kvw), BF16),
        compiler_params=_params("arbitrary"),
        name="attn_k_rope",
    )(proj, cos, sin)


def _head_halves(cat_f32, lo):
    lane = lax.broadcasted_iota(jnp.int32, cat_f32.shape, 1)
    same = jnp.where((lane >= lo) & (lane < lo + HEAD_DIM), cat_f32, 0.0)
    swap = pltpu.roll(same, HEAD_DIM, axis=1)
    low, high = (same, swap) if lo == 0 else (swap, same)
    return low.astype(BF16), high.astype(BF16)


def _attn_kernel(*refs, band, bq, seq, n_ctx, groups):
    if band:
        (sink_ref, q_ref, cos_ref, sin_ref, kp_ref, kc_ref, kn_ref,
         vp_ref, vc_ref, vn_ref, kx_ref, vx_ref, o_ref, cap_scr) = refs
    else:
        sink_ref, q_ref, kx_ref, vx_ref, o_ref = refs
    n_band = bq + 2 * WINDOW
    if band:
        start = pl.program_id(1) * bq
        qpos = start + lax.broadcasted_iota(jnp.int32, (bq, n_band), 0)
        kpos = start - WINDOW + lax.broadcasted_iota(jnp.int32, (bq, n_band), 1)
        valid = (jnp.abs(qpos - kpos) <= WINDOW) & (kpos >= 0) & (kpos < seq)
        cap_scr[...] = jnp.where(valid, jnp.inf, NEG_INF)
        cos = cos_ref[...]
        sin = sin_ref[...]
    lane_q = lax.broadcasted_iota(jnp.int32, (bq, LANES), 1)
    nt = (((1,), (1,)), ((), ()))

    def head(qb, k_ctx, v_ctx, k_band, v_band, sink2):
        s_ctx = lax.dot_general(qb, k_ctx, nt, preferred_element_type=F32)
        m = jnp.maximum(jnp.max(s_ctx, axis=-1, keepdims=True), sink2)
        if band:
            s_band = jnp.minimum(lax.dot_general(qb, k_band, nt, preferred_element_type=F32), cap_scr[...])
            m = jnp.maximum(m, jnp.max(s_band, axis=-1, keepdims=True))
        p_ctx = jnp.exp2(s_ctx - m)
        denom = jnp.sum(p_ctx, axis=-1, keepdims=True) + jnp.exp2(sink2 - m)
        o = jnp.dot(p_ctx.astype(BF16), v_ctx, preferred_element_type=F32)
        if band:
            p_band = jnp.exp2(s_band - m)
            denom = denom + jnp.sum(p_band, axis=-1, keepdims=True)
            o = o + jnp.dot(p_band.astype(BF16), v_band, preferred_element_type=F32)
        return o, 1.0 / denom

    for h in range(N_KV_HEADS):
        pair_cols = slice((h // 2) * LANES, (h // 2 + 1) * LANES)
        lo = (h % 2) * HEAD_DIM
        kx_lo, kx_hi = _head_halves(kx_ref[:, pair_cols].astype(F32), lo)
        vx_lo, vx_hi = _head_halves(vx_ref[:, pair_cols].astype(F32), lo)
        kb_lo = kb_hi = vb_lo = vb_hi = None
        if band:
            kcat = jnp.concatenate([r[:, pair_cols].astype(F32) for r in (kp_ref, kc_ref, kn_ref)], axis=0)
            vcat = jnp.concatenate([r[:, pair_cols].astype(F32) for r in (vp_ref, vc_ref, vn_ref)], axis=0)
            kb_lo, kb_hi = _head_halves(kcat, lo)
            vb_lo, vb_hi = _head_halves(vcat, lo)
        for p in range(groups // 2):
            off = (h * groups + 2 * p) * HEAD_DIM
            cols = slice(off, off + LANES)
            qf = q_ref[:, cols].astype(F32)
            if band:
                qf = _rope128(qf, cos, sin)
            qb = (qf * (ATTN_SCALE * LOG2_E)).astype(BF16)
            o0, inv0 = head(qb, kx_lo, vx_lo, kb_lo, vb_lo, sink_ref[h * groups + 2 * p] * LOG2_E)
            o1, inv1 = head(qb, kx_hi, vx_hi, kb_hi, vb_hi, sink_ref[h * groups + 2 * p + 1] * LOG2_E)
            o_ref[:, cols] = ((o0 + o1) * jnp.where(lane_q < HEAD_DIM, inv0, inv1)).astype(BF16)


def _attention(sink, proj, q_col, v_col, k_rot, proj_c, kc_col, vc_col, b, seq, n_ctx, d, cos, sin):
    kvw = N_KV_HEADS * HEAD_DIM
    groups = d // kvw
    bq = min(256, seq)
    assert seq % bq == 0 and bq % WINDOW == 0 and groups % 2 == 0 and kvw % LANES == 0
    nq = seq // bq
    wpb = bq // WINDOW
    n_halo = b * seq // WINDOW

    def prev_map(col):
        return lambda bi, i: (jnp.maximum((bi * nq + i) * wpb - 1, 0), col)

    def next_map(col):
        return lambda bi, i: (jnp.minimum((bi * nq + i + 1) * wpb, n_halo - 1), col)

    def cur_map(col):
        return lambda bi, i: (bi * nq + i, col)

    return pl.pallas_call(
        functools.partial(_attn_kernel, band=True, bq=bq, seq=seq, n_ctx=n_ctx, groups=groups),
        grid=(b, nq),
        in_specs=[
            pl.BlockSpec(memory_space=pltpu.SMEM),
            pl.BlockSpec((bq, d), cur_map(q_col)),
            pl.BlockSpec((bq, LANES), lambda bi, i: (i, 0)),
            pl.BlockSpec((bq, LANES), lambda bi, i: (i, 0)),
            pl.BlockSpec((WINDOW, kvw), prev_map(0)),
            pl.BlockSpec((bq, kvw), cur_map(0)),
            pl.BlockSpec((WINDOW, kvw), next_map(0)),
            pl.BlockSpec((WINDOW, kvw), prev_map(v_col)),
            pl.BlockSpec((bq, kvw), cur_map(v_col)),
            pl.BlockSpec((WINDOW, kvw), next_map(v_col)),
            pl.BlockSpec((n_ctx, kvw), lambda bi, i: (bi, kc_col)),
            pl.BlockSpec((n_ctx, kvw), lambda bi, i: (bi, vc_col)),
        ],
        out_specs=pl.BlockSpec((bq, d), lambda bi, i: (bi * nq + i, 0)),
        out_shape=jax.ShapeDtypeStruct((b * seq, d), BF16),
        scratch_shapes=[pltpu.VMEM((bq, bq + 2 * WINDOW), F32)],
        compiler_params=_params("arbitrary", "arbitrary"),
        name="window_attention",
    )(sink, proj, cos, sin, k_rot, k_rot, k_rot, proj, proj, proj, proj_c, proj_c)


def _ctx_attention(sink, proj_c, q_col, k_col, v_col, b, n_ctx, d):
    kvw = N_KV_HEADS * HEAD_DIM
    groups = d // kvw
    return pl.pallas_call(
        functools.partial(_attn_kernel, band=False, bq=n_ctx, seq=n_ctx, n_ctx=n_ctx, groups=groups),
        grid=(b,),
        in_specs=[
            pl.BlockSpec(memory_space=pltpu.SMEM),
            pl.BlockSpec((n_ctx, d), lambda bi: (bi, q_col)),
            pl.BlockSpec((n_ctx, kvw), lambda bi: (bi, k_col)),
            pl.BlockSpec((n_ctx, kvw), lambda bi: (bi, v_col)),
        ],
        out_specs=pl.BlockSpec((n_ctx, d), lambda bi: (bi, 0)),
        out_shape=jax.ShapeDtypeStruct((b * n_ctx, d), BF16),
        compiler_params=_params("arbitrary"),
        name="context_attention",
    )(sink, proj_c, proj_c, proj_c)


def _conv_kernel(a_ref, ag_ref, ap_ref, agp_ref, an_ref, agn_ref, w_ref, b_ref, lg_ref, lb_ref,
                 o_ref, g_scr, sh_scr, c_scr, wb_scr, *, tiles_per_seq, kw):
    tm, d = a_ref.shape
    halo = CONV_HALO
    span = tm + 2 * halo
    i = pl.program_id(0)
    first = (i % tiles_per_seq) == 0
    last = (i % tiles_per_seq) == tiles_per_seq - 1

    def glu(a, ag):
        return a.astype(F32) * jax.nn.sigmoid(ag.astype(F32))

    g_scr[0:halo, :] = jnp.where(first, 0.0, glu(ap_ref[...], agp_ref[...]))

    def glu_body(r, carry):
        rows = pl.ds(pl.multiple_of(r * ROW_CHUNK, ROW_CHUNK), ROW_CHUNK)
        g_scr[pl.ds(pl.multiple_of(halo + r * ROW_CHUNK, ROW_CHUNK), ROW_CHUNK), :] = glu(a_ref[rows, :], ag_ref[rows, :])
        return carry

    lax.fori_loop(0, tm // ROW_CHUNK, glu_body, 0, unroll=ROW_UNROLL)
    g_scr[halo + tm:span, :] = jnp.where(last, 0.0, glu(an_ref[...], agn_ref[...]))
    g_scr[span:span + SUBLANES, :] = jnp.zeros((SUBLANES, d), F32)

    for r in range(1, SUBLANES):
        sh_scr[r - 1] = g_scr[r:r + span, :]

    def shifted(r, rows, cols):
        return g_scr[rows, cols] if r == 0 else sh_scr[r - 1, rows, cols]

    pad = kw // 2
    lw = 4 * LANES
    bias = b_ref[...]
    lg = lg_ref[...]
    lb = lb_ref[...]

    @pl.when(i == 0)
    def _():
        for j in range(kw):
            wb_scr[j] = jnp.broadcast_to(w_ref[j:j + 1, :], (SUBLANES, d))

    groups = ROW_CHUNK // SUBLANES

    def row_body(rc, carry):
        base = pl.multiple_of(rc * ROW_CHUNK, ROW_CHUNK)
        for ch in range(d // lw):
            cols = slice(ch * lw, (ch + 1) * lw)
            acc = [jnp.zeros((SUBLANES, lw), F32) for _ in range(groups)]
            for j in range(kw):
                off = j + halo - pad
                w = wb_scr[j, :, cols]
                for s in range(groups):
                    rows = pl.ds(pl.multiple_of(base + (off // SUBLANES + s) * SUBLANES, SUBLANES), SUBLANES)
                    acc[s] = acc[s] + w * shifted(off % SUBLANES, rows, cols)
            for s in range(groups):
                rows = pl.ds(pl.multiple_of(base + s * SUBLANES, SUBLANES), SUBLANES)
                c_scr[rows, cols] = acc[s] + bias[:, cols]
        y = c_scr[pl.ds(base, ROW_CHUNK), :]
        mu = jnp.mean(y, axis=-1, keepdims=True)
        var = jnp.mean(jnp.square(y - mu), axis=-1, keepdims=True)
        yn = (y - mu) * lax.rsqrt(var + NORM_EPS) * lg + lb
        o_ref[pl.ds(base, ROW_CHUNK), :] = _silu(yn).astype(BF16)
        return carry

    lax.fori_loop(0, tm // ROW_CHUNK, row_body, 0)


def _conv_module(proj, seq, d, dw_w, dw_b, ln_g, ln_b):
    t = proj.shape[0]
    kw = dw_w.shape[0]
    tm = min(256, seq)
    halo = CONV_HALO
    assert seq % tm == 0 and tm % halo == 0 and kw // 2 <= halo and d % (4 * LANES) == 0
    tiles_per_seq = seq // tm
    hpt = tm // halo
    n_halo = t // halo
    span = tm + 2 * halo

    def prev_map(col):
        return lambda i: (jnp.maximum(i * hpt - 1, 0), col)

    def next_map(col):
        return lambda i: (jnp.minimum((i + 1) * hpt, n_halo - 1), col)

    vec = lambda: pl.BlockSpec((1, d), lambda i: (0, 0))
    return pl.pallas_call(
        functools.partial(_conv_kernel, tiles_per_seq=tiles_per_seq, kw=kw),
        grid=(t // tm,),
        in_specs=[
            pl.BlockSpec((tm, d), lambda i: (i, 0)),
            pl.BlockSpec((tm, d), lambda i: (i, 1)),
            pl.BlockSpec((halo, d), prev_map(0)),
            pl.BlockSpec((halo, d), prev_map(1)),
            pl.BlockSpec((halo, d), next_map(0)),
            pl.BlockSpec((halo, d), next_map(1)),
            pl.BlockSpec((kw, d), lambda i: (0, 0)),
            vec(), vec(), vec(),
        ],
        out_specs=pl.BlockSpec((tm, d), lambda i: (i, 0)),
        out_shape=jax.ShapeDtypeStruct((t, d), BF16),
        scratch_shapes=[
            pltpu.VMEM((span + SUBLANES, d), F32),
            pltpu.VMEM((SUBLANES - 1, span, d), F32),
            pltpu.VMEM((tm, d), F32),
            pltpu.VMEM((kw, SUBLANES, d), F32),
        ],
        compiler_params=_params("arbitrary"),
        name="conformer_conv",
    )(proj, proj, proj, proj, proj, proj, dw_w, dw_b.reshape(1, d), ln_g.reshape(1, d), ln_b.reshape(1, d))


def kernel(x, c, ctx, c_ctx, norm_g, ada_w, ada_b, four_w_in, four_w_out, attn_w_in, attn_sink,
           attn_w_out, conv_w_in, conv_dw_w, conv_dw_b, conv_ln_g, conv_ln_b, conv_w_out, final_g):
    b, seq, d = x.shape
    n_ctx = ctx.shape[1]
    depth = norm_g.shape[0]
    kvw = N_KV_HEADS * HEAD_DIM
    assert d % kvw == 0 and seq % GRID_W == 0

    n_rows = -(-(b + 1) // SUBLANES) * SUBLANES
    cond = jnp.concatenate([c, c_ctx[None, :], jnp.zeros((n_rows - b - 1, d), F32)], axis=0)
    mod = _modulation(cond, ada_w, ada_b)

    cos, sin = _rope_tables(seq)
    qe = d
    ke = qe + kvw
    ve = ke + kvw
    attn_w_perm = jnp.concatenate([attn_w_in[:, :, ve:], attn_w_in[:, :, :ve]], axis=-1)

    x2 = x.reshape(b * seq, d)
    c2 = ctx.reshape(b * n_ctx, d)
    for i in range(depth):
        kind, j = i % N_MIXERS, i // N_MIXERS
        with_ctx = i < depth - 1
        last = i == depth - 1
        shift = mod[i, :, 0:d].reshape(n_rows, 1, d)
        scale = mod[i, :, d:2 * d].reshape(n_rows, 1, d)
        gate = mod[i, :, 2 * d:3 * d].reshape(n_rows, 1, d)
        if kind == 0:
            w_in = four_w_in[j].astype(BF16)
            w_out = four_w_out[j].astype(BF16)
            proj = _in_proj(x2, norm_g[i], shift, scale, w_in, seq, 0)
            a = _fourier_mixed(proj, b, seq, d, two_stage=True)
            z_col = 1
            if with_ctx:
                proj_c = _in_proj(c2, norm_g[i], shift, scale, w_in, b * n_ctx, b)
                a_c = _fourier_mixed(proj_c, b, n_ctx, d, two_stage=False)
        elif kind == 1:
            w_in = attn_w_perm[j].astype(BF16)
            w_out = attn_w_out[j].astype(BF16)
            z_col, q_col = 0, 1
            k_col = 2 * d // kvw
            v_col = k_col + 1
            proj = _in_proj(x2, norm_g[i], shift, scale, w_in, seq, 0)
            proj_c = _in_proj(c2, norm_g[i], shift, scale, w_in, b * n_ctx, b)
            k_rot = _k_rope(proj, k_col, kvw, seq, cos, sin)
            a = _attention(attn_sink[j], proj, q_col, v_col, k_rot, proj_c, k_col, v_col,
                           b, seq, n_ctx, d, cos, sin)
            if with_ctx:
                a_c = _ctx_attention(attn_sink[j], proj_c, q_col, k_col, v_col, b, n_ctx, d)
        else:
            w_in = conv_w_in[j].astype(BF16)
            w_out = conv_w_out[j].astype(BF16)
            z_col = 2
            conv_args = (conv_dw_w[j], conv_dw_b[j], conv_ln_g[j], conv_ln_b[j])
            proj = _in_proj(x2, norm_g[i], shift, scale, w_in, seq, 0)
            a = _conv_module(proj, seq, d, *conv_args)
            if with_ctx:
                proj_c = _in_proj(c2, norm_g[i], shift, scale, w_in, b * n_ctx, b)
                a_c = _conv_module(proj_c, n_ctx, d, *conv_args)
        x2 = _out_proj(a, proj, z_col, x2, gate, w_out, final_g, seq, 0, final_norm=last)
        if with_ctx:
            c2 = _out_proj(a_c, proj_c, z_col, c2, gate, w_out, final_g, b * n_ctx, b, final_norm=False)
    return x2.reshape(b, seq, d)
```

```python
import functools
import math

import jax
import jax.numpy as jnp
from jax import lax
from jax.experimental import pallas as pl
from jax.experimental.pallas import tpu as pltpu

HEAD_DIM = 64
N_KV_HEADS = 4
FOURIER_GROUPS = 8
WINDOW = 128
GRID_W = 64
N_MIXERS = 3
NORM_EPS = 1e-6
ROPE_BASE = 10000.0
NEG_INF = -1e30
ATTN_SCALE = HEAD_DIM ** -0.5
LOG2_E = math.log2(math.e)

LANES = 128
SUBLANES = 8
VMEM_LIMIT_BYTES = 56 * 1024 * 1024

DFT_N1 = 128
IN_PROJ_COL_TILES = 4
ROW_CHUNK = 16
ROW_UNROLL = 4
CONV_HALO = 16

F32 = jnp.float32
BF16 = jnp.bfloat16


def _params(*sem):
    return pltpu.CompilerParams(dimension_semantics=sem, vmem_limit_bytes=VMEM_LIMIT_BYTES)


def _silu(v):
    return v * jax.nn.sigmoid(v)


def _pick_tile(n, cap, quantum):
    if n <= cap:
        return n
    t = cap - cap % quantum
    while n % t:
        t -= quantum
    return t


def _mod_kernel(cond_ref, w_ref, b_ref, o_ref):
    s = _silu(cond_ref[...]).astype(BF16)
    o_ref[0] = jnp.dot(s, w_ref[0].astype(BF16), preferred_element_type=F32) + b_ref[0]


def _modulation(cond, ada_w, ada_b):
    depth, d, d3 = ada_w.shape
    r = cond.shape[0]
    tn = _pick_tile(d3, 1024, 2 * LANES)
    return pl.pallas_call(
        _mod_kernel,
        grid=(depth, d3 // tn),
        in_specs=[
            pl.BlockSpec((r, d), lambda l, j: (0, 0)),
            pl.BlockSpec((1, d, tn), lambda l, j: (l, 0, j)),
            pl.BlockSpec((1, 1, tn), lambda l, j: (l, 0, j)),
        ],
        out_specs=pl.BlockSpec((1, r, tn), lambda l, j: (l, 0, j)),
        out_shape=jax.ShapeDtypeStruct((depth, r, d3), F32),
        compiler_params=_params("arbitrary", "arbitrary"),
        name="ada_modulation",
    )(cond, ada_w, ada_b.reshape(depth, 1, d3))


def _in_proj_kernel(x_ref, g_ref, shift_ref, scale_ref, w_ref, o_ref, h0_scr, h1_scr):
    i = pl.program_id(0)
    j = pl.program_id(1)
    r = x_ref.shape[0]

    @pl.when((i == 0) & (j == 0))
    def _():
        h1_scr[...] = jnp.zeros(h1_scr.shape, BF16)

    def step(h_write, h_read):
        g = g_ref[...]
        sh = shift_ref[0]
        sc = 1.0 + scale_ref[0]
        for c in range(r // ROW_CHUNK):
            xf = x_ref[c * ROW_CHUNK:(c + 1) * ROW_CHUNK, :]
            ms = jnp.mean(xf * xf, axis=-1, keepdims=True)
            y = xf * lax.rsqrt(ms + NORM_EPS)
            rows = pl.ds(pl.multiple_of(j * r + c * ROW_CHUNK, ROW_CHUNK), ROW_CHUNK)
            h_write[rows, :] = ((y * g) * sc + sh).astype(BF16)
        o_ref[...] = jnp.dot(h_read[...], w_ref[...], preferred_element_type=F32).astype(BF16)

    @pl.when(i % 2 == 0)
    def _():
        step(h0_scr, h1_scr)

    @pl.when(i % 2 == 1)
    def _():
        step(h1_scr, h0_scr)


def _in_proj(x2, norm_g, shift, scale, w_bf16, rows_per_mod, mod_row0):
    t, d = x2.shape
    n_out = w_bf16.shape[1]
    tm = min(1024, rows_per_mod, t)
    nn = IN_PROJ_COL_TILES
    tn = n_out // nn
    r = tm // nn
    assert t % tm == 0 and rows_per_mod % tm == 0 and n_out % nn == 0 and tn % LANES == 0 and r % ROW_CHUNK == 0
    nm = t // tm
    tiles_per_mod = rows_per_mod // tm
    norm_tile = lambda i: jnp.minimum(i, nm - 1)
    mod_map = lambda i, j: (mod_row0 + norm_tile(i) // tiles_per_mod, 0, 0)
    return pl.pallas_call(
        _in_proj_kernel,
        grid=(nm + 1, nn),
        in_specs=[
            pl.BlockSpec((r, d), lambda i, j: (norm_tile(i) * nn + j, 0)),
            pl.BlockSpec((1, d), lambda i, j: (0, 0)),
            pl.BlockSpec((1, 1, d), mod_map),
            pl.BlockSpec((1, 1, d), mod_map),
            pl.BlockSpec((d, tn), lambda i, j: (0, j)),
        ],
        out_specs=pl.BlockSpec((tm, tn), lambda i, j: (jnp.where(i == 0, nm, i - 1), j)),
        out_shape=jax.ShapeDtypeStruct((t + tm, n_out), BF16),
        scratch_shapes=[pltpu.VMEM((tm, d), BF16), pltpu.VMEM((tm, d), BF16)],
        compiler_params=_params("arbitrary", "arbitrary"),
        name="norm_in_proj",
    )(x2, norm_g.reshape(1, d), shift, scale, w_bf16)


def _out_proj_kernel(a_ref, z_ref, x_ref, gate_ref, w_ref, fg_ref, o_ref, y0_scr, y1_scr, *, final_norm):
    i = pl.program_id(0)
    tm = x_ref.shape[0]

    @pl.when(i == 0)
    def _():
        y1_scr[...] = jnp.zeros(y1_scr.shape, BF16)

    def step(y_write, y_read):
        for c in range(tm // ROW_CHUNK):
            rows = slice(c * ROW_CHUNK, (c + 1) * ROW_CHUNK)
            y_write[rows, :] = (a_ref[rows, :].astype(F32) * _silu(z_ref[rows, :].astype(F32))).astype(BF16)
        o = jnp.dot(y_read[...], w_ref[...], preferred_element_type=F32)
        xn = x_ref[...] + gate_ref[0] * o
        if final_norm:
            ms = jnp.mean(xn * xn, axis=-1, keepdims=True)
            xn = (xn * lax.rsqrt(ms + NORM_EPS)) * fg_ref[...]
        o_ref[...] = xn

    @pl.when(i % 2 == 0)
    def _():
        step(y0_scr, y1_scr)

    @pl.when(i % 2 == 1)
    def _():
        step(y1_scr, y0_scr)


def _out_proj(a, z_arr, z_col, x2, gate, w_bf16, final_g, rows_per_mod, mod_row0, final_norm):
    t, d = x2.shape
    tm = min(512, rows_per_mod, t)
    assert t % tm == 0 and rows_per_mod % tm == 0
    nm = t // tm
    tiles_per_mod = rows_per_mod // tm
    gate_tile = lambda i: jnp.minimum(i, nm - 1)
    mm_tile = lambda i: jnp.maximum(i - 1, 0)
    return pl.pallas_call(
        functools.partial(_out_proj_kernel, final_norm=final_norm),
        grid=(nm + 1,),
        in_specs=[
            pl.BlockSpec((tm, d), lambda i: (gate_tile(i), 0)),
            pl.BlockSpec((tm, d), lambda i: (gate_tile(i), z_col)),
            pl.BlockSpec((tm, d), lambda i: (mm_tile(i), 0)),
            pl.BlockSpec((1, 1, d), lambda i: (mod_row0 + mm_tile(i) // tiles_per_mod, 0, 0)),
            pl.BlockSpec((d, d), lambda i: (0, 0), pipeline_mode=pl.Buffered(1)),
            pl.BlockSpec((1, d), lambda i: (0, 0)),
        ],
        out_specs=pl.BlockSpec((tm, d), lambda i: (mm_tile(i), 0)),
        out_shape=jax.ShapeDtypeStruct((t, d), F32),
        scratch_shapes=[pltpu.VMEM((tm, d), BF16), pltpu.VMEM((tm, d), BF16)],
        compiler_params=_params("arbitrary"),
        name="gated_out_proj",
    )(a, z_arr, x2, gate, w_bf16, final_g.reshape(1, d))


def _cos_sin(n_rows, n_cols, period):
    r = jnp.arange(n_rows, dtype=jnp.int32)[:, None]
    c = jnp.arange(n_cols, dtype=jnp.int32)[None, :]
    ang = ((r * c) % period).astype(F32) * (2.0 * math.pi / period)
    return jnp.cos(ang), jnp.sin(ang)


def _pack_complex(re, im):
    hi = lax.bitcast_convert_type(re.astype(BF16).astype(F32), jnp.uint32)
    lo = lax.bitcast_convert_type(im.astype(BF16).astype(F32), jnp.uint32)
    return hi | (lo >> 16)


def _unpack_complex(w):
    re = lax.bitcast_convert_type(w & jnp.uint32(0xFFFF0000), F32)
    im = lax.bitcast_convert_type(w << 16, F32)
    return re.astype(BF16), im.astype(BF16)


def _chan_dft_kernel(u_ref, m_ref, z_ref, *, groups):
    gs = m_ref.shape[0]
    for g in range(groups):
        cols = slice(g * gs, (g + 1) * gs)
        r = jnp.dot(u_ref[:, cols], m_ref[...], preferred_element_type=F32)
        z_ref[:, cols] = _pack_complex(r[:, :gs], r[:, gs:])


def _chan_dft(proj, t, d):
    gs = d // FOURIER_GROUPS
    c, s = _cos_sin(gs, gs, gs)
    m = jnp.concatenate([c, -s], axis=1).astype(BF16)
    tm = min(1024, t)
    assert t % tm == 0
    return pl.pallas_call(
        functools.partial(_chan_dft_kernel, groups=FOURIER_GROUPS),
        grid=(t // tm,),
        in_specs=[
            pl.BlockSpec((tm, d), lambda i: (i, 0)),
            pl.BlockSpec((gs, 2 * gs), lambda i: (0, 0)),
        ],
        out_specs=pl.BlockSpec((tm, d), lambda i: (i, 0)),
        out_shape=jax.ShapeDtypeStruct((t, d), jnp.uint32),
        compiler_params=_params("arbitrary"),
        name="fourier_chan_dft",
    )(proj, m)


def _seq_dft1_kernel(z_ref, m_ref, twc_ref, tws_ref, y_ref, zg_scr, yg_scr):
    n1 = z_ref.shape[1]
    nq = z_ref.shape[2]
    twc = twc_ref[0]
    tws = tws_ref[0]
    for q in range(nq):
        zg_scr[q] = z_ref[0, :, q, :]
    for q in range(nq):
        zr, zi = _unpack_complex(zg_scr[q])
        y = jnp.dot(m_ref[...], jnp.concatenate([zr, zi], axis=0), preferred_element_type=F32)
        yr = y[:n1]
        yi = y[n1:]
        c = twc[:, q:q + 1]
        s = tws[:, q:q + 1]
        yg_scr[q] = _pack_complex(yr * c + yi * s, yi * c - yr * s)
    for q in range(nq):
        y_ref[0, :, q, :] = yg_scr[q]


def _seq_dft1(z, b, seq, d):
    n1 = DFT_N1
    n2 = seq // n1
    nq = SUBLANES
    dt = min(1024, d)
    assert seq % n1 == 0 and n2 % nq == 0 and d % dt == 0
    nj = n2 // nq
    c1, s1 = _cos_sin(n1, n1, n1)
    m = jnp.concatenate([jnp.concatenate([c1, s1], axis=1),
                         jnp.concatenate([-s1, c1], axis=1)], axis=0).astype(BF16)
    twc, tws = _cos_sin(n1, n2, seq)
    twc = twc.reshape(n1, nj, nq).transpose(1, 0, 2)
    tws = tws.reshape(n1, nj, nq).transpose(1, 0, 2)
    blk = pl.BlockSpec((1, n1, nq, dt), lambda bi, j, k: (bi, 0, j, k))
    return pl.pallas_call(
        _seq_dft1_kernel,
        grid=(b, nj, d // dt),
        in_specs=[
            blk,
            pl.BlockSpec((2 * n1, 2 * n1), lambda bi, j, k: (0, 0)),
            pl.BlockSpec((1, n1, nq), lambda bi, j, k: (j, 0, 0)),
            pl.BlockSpec((1, n1, nq), lambda bi, j, k: (j, 0, 0)),
        ],
        out_specs=blk,
        out_shape=jax.ShapeDtypeStruct((b, n1, n2, d), jnp.uint32),
        scratch_shapes=[pltpu.VMEM((nq, n1, dt), jnp.uint32), pltpu.VMEM((nq, n1, dt), jnp.uint32)],
        compiler_params=_params("arbitrary", "arbitrary", "arbitrary"),
        name="fourier_seq_dft_stage1",
    )(z.reshape(b, n1, n2, d), m, twc, tws)


def _seq_dft2_kernel(y_ref, m_ref, o_ref, *, inv_norm):
    for q in range(y_ref.shape[1]):
        yr, yi = _unpack_complex(y_ref[0, q])
        r = jnp.dot(m_ref[...], jnp.concatenate([yr, yi], axis=0), preferred_element_type=F32)
        o_ref[0, :, q, :] = r * inv_norm


def _seq_dft2(y, inv_norm):
    b, n1, n2, d = y.shape
    nq = SUBLANES
    dt = min(1024, d)
    assert n1 % nq == 0 and d % dt == 0
    c2, s2 = _cos_sin(n2, n2, n2)
    m = jnp.concatenate([c2, s2], axis=1).astype(BF16)
    out = pl.pallas_call(
        functools.partial(_seq_dft2_kernel, inv_norm=inv_norm),
        grid=(b, n1 // nq, d // dt),
        in_specs=[
            pl.BlockSpec((1, nq, n2, dt), lambda bi, j, k: (bi, j, 0, k)),
            pl.BlockSpec((n2, 2 * n2), lambda bi, j, k: (0, 0)),
        ],
        out_specs=pl.BlockSpec((1, n2, nq, dt), lambda bi, j, k: (bi, 0, j, k)),
        out_shape=jax.ShapeDtypeStruct((b, n2, n1, d), F32),
        compiler_params=_params("arbitrary", "arbitrary", "arbitrary"),
        name="fourier_seq_dft_stage2",
    )(y, m)
    return out.reshape(b * n2 * n1, d)


def _seq_dft_direct_kernel(z_ref, m_ref, o_ref, *, inv_norm):
    zr, zi = _unpack_complex(z_ref[0])
    o_ref[0] = jnp.dot(m_ref[...], jnp.concatenate([zr, zi], axis=0), preferred_element_type=F32) * inv_norm


def _seq_dft_direct(z, b, seq, d, inv_norm):
    dt = min(1024, d)
    c, s = _cos_sin(seq, seq, seq)
    m = jnp.concatenate([c, s], axis=1).astype(BF16)
    out = pl.pallas_call(
        functools.partial(_seq_dft_direct_kernel, inv_norm=inv_norm),
        grid=(b, d // dt),
        in_specs=[
            pl.BlockSpec((1, seq, dt), lambda bi, k: (bi, 0, k)),
            pl.BlockSpec((seq, 2 * seq), lambda bi, k: (0, 0)),
        ],
        out_specs=pl.BlockSpec((1, seq, dt), lambda bi, k: (bi, 0, k)),
        out_shape=jax.ShapeDtypeStruct((b, seq, d), F32),
        compiler_params=_params("arbitrary", "arbitrary"),
        name="fourier_seq_dft_direct",
    )(z.reshape(b, seq, d), m)
    return out.reshape(b * seq, d)


def _fourier_mixed(proj, b, seq, d, two_stage):
    z = _chan_dft(proj, b * seq, d)
    inv_norm = 1.0 / math.sqrt(seq * (d // FOURIER_GROUPS))
    if two_stage:
        return _seq_dft2(_seq_dft1(z, b, seq, d), inv_norm)
    return _seq_dft_direct(z, b, seq, d, inv_norm)


def _rope_tables(n_tokens):
    rows = n_tokens // GRID_W
    row = jnp.repeat(jnp.arange(rows), GRID_W).astype(F32)
    col = jnp.tile(jnp.arange(GRID_W), rows).astype(F32)
    quarter = HEAD_DIM // 4
    inv_freq = ROPE_BASE ** (-jnp.arange(quarter, dtype=F32) / quarter)
    ang_r = row[:, None] * inv_freq[None, :]
    ang_c = col[:, None] * inv_freq[None, :]
    ang = jnp.concatenate([ang_r, ang_r, ang_c, ang_c], axis=-1)
    sign = jnp.where((jnp.arange(HEAD_DIM) % (HEAD_DIM // 2)) < quarter, -1.0, 1.0).astype(F32)
    reps = LANES // HEAD_DIM
    return jnp.tile(jnp.cos(ang), (1, reps)), jnp.tile(jnp.sin(ang) * sign, (1, reps))


def _rope128(xf, cos, sin_signed):
    quarter = HEAD_DIM // 4
    lane = lax.broadcasted_iota(jnp.int32, xf.shape, 1)
    from_hi = pltpu.roll(xf, LANES - quarter, axis=1)
    from_lo = pltpu.roll(xf, quarter, axis=1)
    partner = jnp.where((lane % (HEAD_DIM // 2)) < quarter, from_hi, from_lo)
    return xf * cos + partner * sin_signed


def _k_rope_kernel(k_ref, cos_ref, sin_ref, o_ref):
    cos = cos_ref[...]
    sin = sin_ref[...]
    for c in range(k_ref.shape[1] // LANES):
        cols = slice(c * LANES, (c + 1) * LANES)
        o_ref[:, cols] = _rope128(k_ref[:, cols].astype(F32), cos, sin).astype(BF16)


def _k_rope(proj, t, k_col, kvw, seq, cos, sin):
    tm = min(1024, seq)
    assert seq % tm == 0 and kvw % LANES == 0
    tiles_per_seq = seq // tm
    return pl.pallas_call(
        _k_rope_kernel,
        grid=(t // tm,),
        in_specs=[
            pl.BlockSpec((tm, kvw), lambda i: (i, k_col)),
            pl.BlockSpec((tm, LANES), lambda i: (i % tiles_per_seq, 0)),
            pl.BlockSpec((tm, LANES), lambda i: (i % tiles_per_seq, 0)),
        ],
        out_specs=pl.BlockSpec((tm, kvw), lambda i: (i, 0)),
        out_shape=jax.ShapeDtypeStruct((t, kvw), BF16),
        compiler_params=_params("arbitrary"),
        name="attn_k_rope",
    )(proj, cos, sin)


def _head_halves(cat_f32, lo):
    lane = lax.broadcasted_iota(jnp.int32, cat_f32.shape, 1)
    same = jnp.where((lane >= lo) & (lane < lo + HEAD_DIM), cat_f32, 0.0)
    swap = pltpu.roll(same, HEAD_DIM, axis=1)
    low, high = (same, swap) if lo == 0 else (swap, same)
    return low.astype(BF16), high.astype(BF16)


def _attn_kernel(*refs, band, bq, seq, n_ctx, groups):
    if band:
        (sink_ref, q_ref, cos_ref, sin_ref, kp_ref, kc_ref, kn_ref,
         vp_ref, vc_ref, vn_ref, kx_ref, vx_ref, o_ref, cap_scr) = refs
    else:
        sink_ref, q_ref, kx_ref, vx_ref, o_ref = refs
    n_band = bq + 2 * WINDOW if band else 0
    nk = n_band + n_ctx
    if band:
        start = pl.program_id(1) * bq
        qpos = start + lax.broadcasted_iota(jnp.int32, (bq, nk), 0)
        col = lax.broadcasted_iota(jnp.int32, (bq, nk), 1)
        kpos = start - WINDOW + col
        valid = ((jnp.abs(qpos - kpos) <= WINDOW) & (kpos >= 0) & (kpos < seq)) | (col >= n_band)
        cap_scr[...] = jnp.where(valid, jnp.inf, NEG_INF)
        cos = cos_ref[...]
        sin = sin_ref[...]
    lane_q = lax.broadcasted_iota(jnp.int32, (bq, LANES), 1)
    nt = (((1,), (1,)), ((), ()))

    def probs(s, sink2):
        if band:
            s = jnp.minimum(s, cap_scr[...])
        m = jnp.maximum(jnp.max(s, axis=-1, keepdims=True), sink2)
        p = jnp.exp2(s - m)
        denom = jnp.sum(p, axis=-1, keepdims=True) + jnp.exp2(sink2 - m)
        return p.astype(BF16), 1.0 / denom

    for h in range(N_KV_HEADS):
        pair_cols = slice((h // 2) * LANES, (h // 2 + 1) * LANES)
        k_refs = (kp_ref, kc_ref, kn_ref, kx_ref) if band else (kx_ref,)
        v_refs = (vp_ref, vc_ref, vn_ref, vx_ref) if band else (vx_ref,)
        kcat = jnp.concatenate([r[:, pair_cols].astype(F32) for r in k_refs], axis=0)
        vcat = jnp.concatenate([r[:, pair_cols].astype(F32) for r in v_refs], axis=0)
        lo = (h % 2) * HEAD_DIM
        k_lo, k_hi = _head_halves(kcat, lo)
        v_lo, v_hi = _head_halves(vcat, lo)
        for p in range(groups // 2):
            off = (h * groups + 2 * p) * HEAD_DIM
            cols = slice(off, off + LANES)
            qf = q_ref[:, cols].astype(F32)
            if band:
                qf = _rope128(qf, cos, sin)
            qb = (qf * (ATTN_SCALE * LOG2_E)).astype(BF16)
            s0 = lax.dot_general(qb, k_lo, nt, preferred_element_type=F32)
            s1 = lax.dot_general(qb, k_hi, nt, preferred_element_type=F32)
            p0, inv0 = probs(s0, sink_ref[h * groups + 2 * p] * LOG2_E)
            p1, inv1 = probs(s1, sink_ref[h * groups + 2 * p + 1] * LOG2_E)
            o_pair = (jnp.dot(p0, v_lo, preferred_element_type=F32)
                      + jnp.dot(p1, v_hi, preferred_element_type=F32))
            o_ref[:, cols] = (o_pair * jnp.where(lane_q < HEAD_DIM, inv0, inv1)).astype(BF16)


def _attention(sink, proj, q_col, v_col, k_rot, proj_c, kc_col, vc_col, b, seq, n_ctx, d, cos, sin):
    kvw = N_KV_HEADS * HEAD_DIM
    groups = d // kvw
    bq = min(256, seq)
    assert seq % bq == 0 and bq % WINDOW == 0 and groups % 2 == 0 and kvw % LANES == 0
    nq = seq // bq
    wpb = bq // WINDOW
    n_halo = b * seq // WINDOW

    def prev_map(col):
        return lambda bi, i: (jnp.maximum((bi * nq + i) * wpb - 1, 0), col)

    def next_map(col):
        return lambda bi, i: (jnp.minimum((bi * nq + i + 1) * wpb, n_halo - 1), col)

    def cur_map(col):
        return lambda bi, i: (bi * nq + i, col)

    return pl.pallas_call(
        functools.partial(_attn_kernel, band=True, bq=bq, seq=seq, n_ctx=n_ctx, groups=groups),
        grid=(b, nq),
        in_specs=[
            pl.BlockSpec(memory_space=pltpu.SMEM),
            pl.BlockSpec((bq, d), cur_map(q_col)),
            pl.BlockSpec((bq, LANES), lambda bi, i: (i, 0)),
            pl.BlockSpec((bq, LANES), lambda bi, i: (i, 0)),
            pl.BlockSpec((WINDOW, kvw), prev_map(0)),
            pl.BlockSpec((bq, kvw), cur_map(0)),
            pl.BlockSpec((WINDOW, kvw), next_map(0)),
            pl.BlockSpec((WINDOW, kvw), prev_map(v_col)),
            pl.BlockSpec((bq, kvw), cur_map(v_col)),
            pl.BlockSpec((WINDOW, kvw), next_map(v_col)),
            pl.BlockSpec((n_ctx, kvw), lambda bi, i: (bi, kc_col)),
            pl.BlockSpec((n_ctx, kvw), lambda bi, i: (bi, vc_col)),
        ],
        out_specs=pl.BlockSpec((bq, d), lambda bi, i: (bi * nq + i, 0)),
        out_shape=jax.ShapeDtypeStruct((b * seq, d), BF16),
        scratch_shapes=[pltpu.VMEM((bq, bq + 2 * WINDOW + n_ctx), F32)],
        compiler_params=_params("arbitrary", "arbitrary"),
        name="window_attention",
    )(sink, proj, cos, sin, k_rot, k_rot, k_rot, proj, proj, proj, proj_c, proj_c)


def _ctx_attention(sink, proj_c, q_col, k_col, v_col, b, n_ctx, d):
    kvw = N_KV_HEADS * HEAD_DIM
    groups = d // kvw
    return pl.pallas_call(
        functools.partial(_attn_kernel, band=False, bq=n_ctx, seq=n_ctx, n_ctx=n_ctx, groups=groups),
        grid=(b,),
        in_specs=[
            pl.BlockSpec(memory_space=pltpu.SMEM),
            pl.BlockSpec((n_ctx, d), lambda bi: (bi, q_col)),
            pl.BlockSpec((n_ctx, kvw), lambda bi: (bi, k_col)),
            pl.BlockSpec((n_ctx, kvw), lambda bi: (bi, v_col)),
        ],
        out_specs=pl.BlockSpec((n_ctx, d), lambda bi: (bi, 0)),
        out_shape=jax.ShapeDtypeStruct((b * n_ctx, d), BF16),
        compiler_params=_params("arbitrary"),
        name="context_attention",
    )(sink, proj_c, proj_c, proj_c)


def _conv_kernel(a_ref, ag_ref, ap_ref, agp_ref, an_ref, agn_ref, w_ref, b_ref, lg_ref, lb_ref,
                 o_ref, g_scr, sh_scr, c_scr, wb_scr, *, tiles_per_seq, kw):
    tm, d = a_ref.shape
    halo = CONV_HALO
    span = tm + 2 * halo
    i = pl.program_id(0)
    first = (i % tiles_per_seq) == 0
    last = (i % tiles_per_seq) == tiles_per_seq - 1

    def glu(a, ag):
        return a.astype(F32) * jax.nn.sigmoid(ag.astype(F32))

    g_scr[0:halo, :] = jnp.where(first, 0.0, glu(ap_ref[...], agp_ref[...]))

    def glu_body(r, carry):
        rows = pl.ds(pl.multiple_of(r * ROW_CHUNK, ROW_CHUNK), ROW_CHUNK)
        g_scr[pl.ds(pl.multiple_of(halo + r * ROW_CHUNK, ROW_CHUNK), ROW_CHUNK), :] = glu(a_ref[rows, :], ag_ref[rows, :])
        return carry

    lax.fori_loop(0, tm // ROW_CHUNK, glu_body, 0, unroll=ROW_UNROLL)
    g_scr[halo + tm:span, :] = jnp.where(last, 0.0, glu(an_ref[...], agn_ref[...]))
    g_scr[span:span + SUBLANES, :] = jnp.zeros((SUBLANES, d), F32)

    for r in range(1, SUBLANES):
        sh_scr[r - 1] = g_scr[r:r + span, :]

    def shifted(r, rows, cols):
        return g_scr[rows, cols] if r == 0 else sh_scr[r - 1, rows, cols]

    pad = kw // 2
    lw = 4 * LANES
    bias = b_ref[...]
    lg = lg_ref[...]
    lb = lb_ref[...]

    @pl.when(i == 0)
    def _():
        for j in range(kw):
            wb_scr[j] = jnp.broadcast_to(w_ref[j:j + 1, :], (SUBLANES, d))

    groups = ROW_CHUNK // SUBLANES

    def row_body(rc, carry):
        base = pl.multiple_of(rc * ROW_CHUNK, ROW_CHUNK)
        for ch in range(d // lw):
            cols = slice(ch * lw, (ch + 1) * lw)
            acc = [jnp.zeros((SUBLANES, lw), F32) for _ in range(groups)]
            for j in range(kw):
                off = j + halo - pad
                w = wb_scr[j, :, cols]
                for s in range(groups):
                    rows = pl.ds(pl.multiple_of(base + (off // SUBLANES + s) * SUBLANES, SUBLANES), SUBLANES)
                    acc[s] = acc[s] + w * shifted(off % SUBLANES, rows, cols)
            for s in range(groups):
                rows = pl.ds(pl.multiple_of(base + s * SUBLANES, SUBLANES), SUBLANES)
                c_scr[rows, cols] = acc[s] + bias[:, cols]
        y = c_scr[pl.ds(base, ROW_CHUNK), :]
        mu = jnp.mean(y, axis=-1, keepdims=True)
        var = jnp.mean(jnp.square(y - mu), axis=-1, keepdims=True)
        yn = (y - mu) * lax.rsqrt(var + NORM_EPS) * lg + lb
        o_ref[pl.ds(base, ROW_CHUNK), :] = _silu(yn).astype(BF16)
        return carry

    lax.fori_loop(0, tm // ROW_CHUNK, row_body, 0)


def _conv_module(proj, t, seq, d, dw_w, dw_b, ln_g, ln_b):
    kw = dw_w.shape[0]
    tm = min(256, seq)
    halo = CONV_HALO
    assert seq % tm == 0 and tm % halo == 0 and kw // 2 <= halo and d % (4 * LANES) == 0
    tiles_per_seq = seq // tm
    hpt = tm // halo
    n_halo = t // halo
    span = tm + 2 * halo

    def prev_map(col):
        return lambda i: (jnp.maximum(i * hpt - 1, 0), col)

    def next_map(col):
        return lambda i: (jnp.minimum((i + 1) * hpt, n_halo - 1), col)

    vec = lambda: pl.BlockSpec((1, d), lambda i: (0, 0))
    return pl.pallas_call(
        functools.partial(_conv_kernel, tiles_per_seq=tiles_per_seq, kw=kw),
        grid=(t // tm,),
        in_specs=[
            pl.BlockSpec((tm, d), lambda i: (i, 0)),
            pl.BlockSpec((tm, d), lambda i: (i, 1)),
            pl.BlockSpec((halo, d), prev_map(0)),
            pl.BlockSpec((halo, d), prev_map(1)),
            pl.BlockSpec((halo, d), next_map(0)),
            pl.BlockSpec((halo, d), next_map(1)),
            pl.BlockSpec((kw, d), lambda i: (0, 0)),
            vec(), vec(), vec(),
        ],
        out_specs=pl.BlockSpec((tm, d), lambda i: (i, 0)),
        out_shape=jax.ShapeDtypeStruct((t, d), BF16),
        scratch_shapes=[
            pltpu.VMEM((span + SUBLANES, d), F32),
            pltpu.VMEM((SUBLANES - 1, span, d), F32),
            pltpu.VMEM((tm, d), F32),
            pltpu.VMEM((kw, SUBLANES, d), F32),
        ],
        compiler_params=_params("arbitrary"),
        name="conformer_conv",
    )(proj, proj, proj, proj, proj, proj, dw_w, dw_b.reshape(1, d), ln_g.reshape(1, d), ln_b.reshape(1, d))


def kernel(x, c, ctx, c_ctx, norm_g, ada_w, ada_b, four_w_in, four_w_out, attn_w_in, attn_sink,
           attn_w_out, conv_w_in, conv_dw_w, conv_dw_b, conv_ln_g, conv_ln_b, conv_w_out, final_g):
    b, seq, d = x.shape
    n_ctx = ctx.shape[1]
    depth = norm_g.shape[0]
    kvw = N_KV_HEADS * HEAD_DIM
    assert d % kvw == 0 and seq % GRID_W == 0

    n_rows = -(-(b + 1) // SUBLANES) * SUBLANES
    cond = jnp.concatenate([c, c_ctx[None, :], jnp.zeros((n_rows - b - 1, d), F32)], axis=0)
    mod = _modulation(cond, ada_w, ada_b)

    cos, sin = _rope_tables(seq)
    qe = d
    ke = qe + kvw
    ve = ke + kvw
    attn_w_perm = jnp.concatenate([attn_w_in[:, :, ve:], attn_w_in[:, :, :ve]], axis=-1)

    x2 = x.reshape(b * seq, d)
    c2 = ctx.reshape(b * n_ctx, d)
    for i in range(depth):
        kind, j = i % N_MIXERS, i // N_MIXERS
        with_ctx = i < depth - 1
        last = i == depth - 1
        shift = mod[i, :, 0:d].reshape(n_rows, 1, d)
        scale = mod[i, :, d:2 * d].reshape(n_rows, 1, d)
        gate = mod[i, :, 2 * d:3 * d].reshape(n_rows, 1, d)
        if kind == 0:
            w_in = four_w_in[j].astype(BF16)
            w_out = four_w_out[j].astype(BF16)
            proj = _in_proj(x2, norm_g[i], shift, scale, w_in, seq, 0)
            a = _fourier_mixed(proj, b, seq, d, two_stage=True)
            z_col = 1
            if with_ctx:
                proj_c = _in_proj(c2, norm_g[i], shift, scale, w_in, b * n_ctx, b)
                a_c = _fourier_mixed(proj_c, b, n_ctx, d, two_stage=False)
        elif kind == 1:
            w_in = attn_w_perm[j].astype(BF16)
            w_out = attn_w_out[j].astype(BF16)
            z_col, q_col = 0, 1
            k_col = 2 * d // kvw
            v_col = k_col + 1
            proj = _in_proj(x2, norm_g[i], shift, scale, w_in, seq, 0)
            proj_c = _in_proj(c2, norm_g[i], shift, scale, w_in, b * n_ctx, b)
            k_rot = _k_rope(proj, b * seq, k_col, kvw, seq, cos, sin)
            a = _attention(attn_sink[j], proj, q_col, v_col, k_rot, proj_c, k_col, v_col,
                           b, seq, n_ctx, d, cos, sin)
            if with_ctx:
                a_c = _ctx_attention(attn_sink[j], proj_c, q_col, k_col, v_col, b, n_ctx, d)
        else:
            w_in = conv_w_in[j].astype(BF16)
            w_out = conv_w_out[j].astype(BF16)
            z_col = 2
            conv_args = (conv_dw_w[j], conv_dw_b[j], conv_ln_g[j], conv_ln_b[j])
            proj = _in_proj(x2, norm_g[i], shift, scale, w_in, seq, 0)
            a = _conv_module(proj, b * seq, seq, d, *conv_args)
            if with_ctx:
                proj_c = _in_proj(c2, norm_g[i], shift, scale, w_in, b * n_ctx, b)
                a_c = _conv_module(proj_c, b * n_ctx, n_ctx, d, *conv_args)
        x2 = _out_proj(a, proj, z_col, x2, gate, w_out, final_g, seq, 0, final_norm=last)
        if with_ctx:
            c2 = _out_proj(a_c, proj_c, z_col, c2, gate, w_out, final_g, b * n_ctx, b, final_norm=False)
    return x2.reshape(b, seq, d)
```

```python
import functools
import math

import jax
import jax.numpy as jnp
from jax import lax
from jax.experimental import pallas as pl
from jax.experimental.pallas import tpu as pltpu

HEAD_DIM = 64
N_KV_HEADS = 4
FOURIER_GROUPS = 8
WINDOW = 128
GRID_W = 64
N_MIXERS = 3
NORM_EPS = 1e-6
ROPE_BASE = 10000.0
NEG_INF = -1e30
ATTN_SCALE = HEAD_DIM ** -0.5
LOG2_E = math.log2(math.e)

LANES = 128
SUBLANES = 8
VMEM_LIMIT_BYTES = 56 * 1024 * 1024

DFT_N1 = 128
IN_PROJ_COL_TILES = 4
ROW_CHUNK = 16
ROW_UNROLL = 4
CONV_HALO = 16

F32 = jnp.float32
BF16 = jnp.bfloat16


def _params(*sem):
    return pltpu.CompilerParams(dimension_semantics=sem, vmem_limit_bytes=VMEM_LIMIT_BYTES)


def _silu(v):
    return v * jax.nn.sigmoid(v)


def _pick_tile(n, cap, quantum):
    if n <= cap:
        return n
    t = cap - cap % quantum
    while n % t:
        t -= quantum
    return t


def _mod_kernel(cond_ref, w_ref, b_ref, o_ref):
    s = _silu(cond_ref[...]).astype(BF16)
    o_ref[0] = jnp.dot(s, w_ref[0].astype(BF16), preferred_element_type=F32) + b_ref[0]


def _modulation(cond, ada_w, ada_b):
    depth, d, d3 = ada_w.shape
    r = cond.shape[0]
    tn = _pick_tile(d3, 1024, 2 * LANES)
    return pl.pallas_call(
        _mod_kernel,
        grid=(depth, d3 // tn),
        in_specs=[
            pl.BlockSpec((r, d), lambda l, j: (0, 0)),
            pl.BlockSpec((1, d, tn), lambda l, j: (l, 0, j)),
            pl.BlockSpec((1, 1, tn), lambda l, j: (l, 0, j)),
        ],
        out_specs=pl.BlockSpec((1, r, tn), lambda l, j: (l, 0, j)),
        out_shape=jax.ShapeDtypeStruct((depth, r, d3), F32),
        compiler_params=_params("arbitrary", "arbitrary"),
        name="ada_modulation",
    )(cond, ada_w, ada_b.reshape(depth, 1, d3))


def _in_proj_kernel(x_ref, g_ref, shift_ref, scale_ref, w_ref, o_ref, h0_scr, h1_scr):
    i = pl.program_id(0)
    j = pl.program_id(1)
    r = x_ref.shape[0]

    @pl.when((i == 0) & (j == 0))
    def _():
        h1_scr[...] = jnp.zeros(h1_scr.shape, BF16)

    def step(h_write, h_read):
        o_ref[...] = jnp.dot(h_read[...], w_ref[...], preferred_element_type=F32).astype(BF16)
        g = g_ref[...]
        sh = shift_ref[0]
        sc = 1.0 + scale_ref[0]
        for c in range(r // ROW_CHUNK):
            xf = x_ref[c * ROW_CHUNK:(c + 1) * ROW_CHUNK, :]
            ms = jnp.mean(xf * xf, axis=-1, keepdims=True)
            y = xf * lax.rsqrt(ms + NORM_EPS)
            rows = pl.ds(pl.multiple_of(j * r + c * ROW_CHUNK, ROW_CHUNK), ROW_CHUNK)
            h_write[rows, :] = ((y * g) * sc + sh).astype(BF16)

    @pl.when(i % 2 == 0)
    def _():
        step(h0_scr, h1_scr)

    @pl.when(i % 2 == 1)
    def _():
        step(h1_scr, h0_scr)


def _in_proj(x2, norm_g, shift, scale, w_bf16, rows_per_mod, mod_row0):
    t, d = x2.shape
    n_out = w_bf16.shape[1]
    tm = min(1024, rows_per_mod, t)
    nn = IN_PROJ_COL_TILES
    tn = n_out // nn
    r = tm // nn
    assert t % tm == 0 and rows_per_mod % tm == 0 and n_out % nn == 0 and tn % LANES == 0 and r % ROW_CHUNK == 0
    nm = t // tm
    tiles_per_mod = rows_per_mod // tm
    norm_tile = lambda i: jnp.minimum(i, nm - 1)
    mod_map = lambda i, j: (mod_row0 + norm_tile(i) // tiles_per_mod, 0, 0)
    return pl.pallas_call(
        _in_proj_kernel,
        grid=(nm + 1, nn),
        in_specs=[
            pl.BlockSpec((r, d), lambda i, j: (norm_tile(i) * nn + j, 0)),
            pl.BlockSpec((1, d), lambda i, j: (0, 0)),
            pl.BlockSpec((1, 1, d), mod_map),
            pl.BlockSpec((1, 1, d), mod_map),
            pl.BlockSpec((d, tn), lambda i, j: (0, j)),
        ],
        out_specs=pl.BlockSpec((tm, tn), lambda i, j: (jnp.where(i == 0, nm, i - 1), j)),
        out_shape=jax.ShapeDtypeStruct((t + tm, n_out), BF16),
        scratch_shapes=[pltpu.VMEM((tm, d), BF16), pltpu.VMEM((tm, d), BF16)],
        compiler_params=_params("arbitrary", "arbitrary"),
        name="norm_in_proj",
    )(x2, norm_g.reshape(1, d), shift, scale, w_bf16)


def _out_proj_kernel(a_ref, z_ref, x_ref, gate_ref, w_ref, fg_ref, o_ref, y0_scr, y1_scr, *, final_norm, a_run):
    i = pl.program_id(0)
    tm = x_ref.shape[0]

    @pl.when(i == 0)
    def _():
        y1_scr[...] = jnp.zeros(y1_scr.shape, BF16)

    def step(y_write, y_read):
        o = jnp.dot(y_read[...], w_ref[...], preferred_element_type=F32)
        for c in range(tm // ROW_CHUNK):
            rows = slice(c * ROW_CHUNK, (c + 1) * ROW_CHUNK)
            if len(a_ref.shape) == 3:
                off = (c * ROW_CHUNK) % a_run
                a = a_ref[(c * ROW_CHUNK) // a_run, off:off + ROW_CHUNK, :]
            else:
                a = a_ref[rows, :]
            y_write[rows, :] = (a.astype(F32) * _silu(z_ref[rows, :].astype(F32))).astype(BF16)
        xn = x_ref[...] + gate_ref[0] * o
        if final_norm:
            ms = jnp.mean(xn * xn, axis=-1, keepdims=True)
            xn = (xn * lax.rsqrt(ms + NORM_EPS)) * fg_ref[...]
        o_ref[...] = xn

    @pl.when(i % 2 == 0)
    def _():
        step(y0_scr, y1_scr)

    @pl.when(i % 2 == 1)
    def _():
        step(y1_scr, y0_scr)


def _out_proj(a, z_arr, z_col, x2, gate, w_bf16, final_g, rows_per_mod, mod_row0, final_norm):
    t, d = x2.shape
    tm = min(512, rows_per_mod, t)
    assert t % tm == 0 and rows_per_mod % tm == 0
    nm = t // tm
    tiles_per_mod = rows_per_mod // tm
    gate_tile = lambda i: jnp.minimum(i, nm - 1)
    mm_tile = lambda i: jnp.maximum(i - 1, 0)
    a_run = None
    a_spec = pl.BlockSpec((tm, d), lambda i: (gate_tile(i), 0))
    if a.ndim == 3:
        a_run = t // a.shape[0]
        assert tm % a_run == 0 and a_run % ROW_CHUNK == 0
        a_spec = pl.BlockSpec((tm // a_run, a.shape[1], d), lambda i: (gate_tile(i), 0, 0))
    return pl.pallas_call(
        functools.partial(_out_proj_kernel, final_norm=final_norm, a_run=a_run),
        grid=(nm + 1,),
        in_specs=[
            a_spec,
            pl.BlockSpec((tm, d), lambda i: (gate_tile(i), z_col)),
            pl.BlockSpec((tm, d), lambda i: (mm_tile(i), 0)),
            pl.BlockSpec((1, 1, d), lambda i: (mod_row0 + mm_tile(i) // tiles_per_mod, 0, 0)),
            pl.BlockSpec((d, d), lambda i: (0, 0), pipeline_mode=pl.Buffered(1)),
            pl.BlockSpec((1, d), lambda i: (0, 0)),
        ],
        out_specs=pl.BlockSpec((tm, d), lambda i: (mm_tile(i), 0)),
        out_shape=jax.ShapeDtypeStruct((t, d), F32),
        scratch_shapes=[pltpu.VMEM((tm, d), BF16), pltpu.VMEM((tm, d), BF16)],
        compiler_params=_params("arbitrary"),
        name="gated_out_proj",
    )(a, z_arr, x2, gate, w_bf16, final_g.reshape(1, d))


def _cos_sin(n_rows, n_cols, period):
    r = jnp.arange(n_rows, dtype=jnp.int32)[:, None]
    c = jnp.arange(n_cols, dtype=jnp.int32)[None, :]
    ang = ((r * c) % period).astype(F32) * (2.0 * math.pi / period)
    return jnp.cos(ang), jnp.sin(ang)


def _pack_complex(re, im):
    hi = lax.bitcast_convert_type(re.astype(BF16).astype(F32), jnp.uint32)
    lo = lax.bitcast_convert_type(im.astype(BF16).astype(F32), jnp.uint32)
    return hi | (lo >> 16)


def _unpack_complex(w):
    re = lax.bitcast_convert_type(w & jnp.uint32(0xFFFF0000), F32)
    im = lax.bitcast_convert_type(w << 16, F32)
    return re.astype(BF16), im.astype(BF16)


def _chan_dft_kernel(u_ref, m_ref, z_ref, *, groups, run):
    gs = m_ref.shape[0]
    tm = u_ref.shape[0]
    for g in range(groups):
        cols = slice(g * gs, (g + 1) * gs)
        r = jnp.dot(u_ref[:, cols], m_ref[...], preferred_element_type=F32)
        packed = _pack_complex(r[:, :gs], r[:, gs:])
        if run is None:
            z_ref[:, cols] = packed
        else:
            for k in range(tm // run):
                z_ref[k, 0:run, cols] = packed[k * run:(k + 1) * run]
    if run is not None:
        pad = z_ref.shape[1] - run
        z_ref[:, run:, :] = jnp.zeros((tm // run, pad, z_ref.shape[2]), jnp.uint32)


def _chan_dft(proj, t, d, run=None, pitch=None):
    gs = d // FOURIER_GROUPS
    c, s = _cos_sin(gs, gs, gs)
    m = jnp.concatenate([c, -s], axis=1).astype(BF16)
    tm = min(1024, t)
    assert t % tm == 0
    if run is None:
        out_spec = pl.BlockSpec((tm, d), lambda i: (i, 0))
        out_shape = jax.ShapeDtypeStruct((t, d), jnp.uint32)
    else:
        assert tm % run == 0
        out_spec = pl.BlockSpec((tm // run, pitch, d), lambda i: (i, 0, 0))
        out_shape = jax.ShapeDtypeStruct((t // run, pitch, d), jnp.uint32)
    return pl.pallas_call(
        functools.partial(_chan_dft_kernel, groups=FOURIER_GROUPS, run=run),
        grid=(t // tm,),
        in_specs=[
            pl.BlockSpec((tm, d), lambda i: (i, 0)),
            pl.BlockSpec((gs, 2 * gs), lambda i: (0, 0)),
        ],
        out_specs=out_spec,
        out_shape=out_shape,
        compiler_params=_params("arbitrary"),
        name="fourier_chan_dft",
    )(proj, m)


def _stack_pair(parts):
    return jnp.concatenate([jnp.concatenate([parts[0][0], parts[1][0]], axis=1),
                            jnp.concatenate([parts[0][1], parts[1][1]], axis=1)], axis=0)


def _seq_dft_kernel(z_ref, m1_ref, m2_ref, twc_ref, tws_ref, o_ref, y_scr, *, n1, n2, pitch, opitch, inv_norm):
    twc = twc_ref[...]
    tws = tws_ref[...]
    for q in range(0, n2, 2):
        zin = _stack_pair([_unpack_complex(z_ref[0, pl.ds(qq, n1, stride=pitch), :]) for qq in (q, q + 1)])
        y = jnp.dot(m1_ref[...], zin, preferred_element_type=F32)
        for t, qq in enumerate((q, q + 1)):
            yr = y[:n1, t * LANES:(t + 1) * LANES]
            yi = y[n1:, t * LANES:(t + 1) * LANES]
            c = twc[:, qq:qq + 1]
            s = tws[:, qq:qq + 1]
            y_scr[pl.ds(qq, n1, stride=pitch), :] = _pack_complex(yr * c + yi * s, yi * c - yr * s)
    for k in range(0, n1, 2):
        yin = _stack_pair([_unpack_complex(y_scr[kk * pitch:kk * pitch + n2, :]) for kk in (k, k + 1)])
        r = jnp.dot(m2_ref[...], yin, preferred_element_type=F32) * inv_norm
        for t, kk in enumerate((k, k + 1)):
            o_ref[0, pl.ds(kk, n2, stride=opitch), :] = r[:, t * LANES:(t + 1) * LANES]
    for j in range(n1, opitch):
        o_ref[0, pl.ds(j, n2, stride=opitch), :] = jnp.zeros((n2, LANES), F32)


def _seq_dft(zp, b, seq, d, pitch, opitch, inv_norm):
    n1 = DFT_N1
    n2 = seq // n1
    assert seq % n1 == 0 and n1 % 2 == 0 and n2 % 2 == 0 and d % LANES == 0
    c1, s1 = _cos_sin(n1, n1, n1)
    m1 = jnp.concatenate([jnp.concatenate([c1, s1], axis=1),
                          jnp.concatenate([-s1, c1], axis=1)], axis=0).astype(BF16)
    c2, s2 = _cos_sin(n2, n2, n2)
    m2 = jnp.concatenate([c2, s2], axis=1).astype(BF16)
    twc, tws = _cos_sin(n1, n2, seq)
    const = lambda shape: pl.BlockSpec(shape, lambda bi, k: (0, 0))
    out = pl.pallas_call(
        functools.partial(_seq_dft_kernel, n1=n1, n2=n2, pitch=pitch, opitch=opitch, inv_norm=inv_norm),
        grid=(b, d // LANES),
        in_specs=[
            pl.BlockSpec((1, n1 * pitch, LANES), lambda bi, k: (bi, 0, k)),
            const((2 * n1, 2 * n1)), const((n2, 2 * n2)), const((n1, n2)), const((n1, n2)),
        ],
        out_specs=pl.BlockSpec((1, n2 * opitch, LANES), lambda bi, k: (bi, 0, k)),
        out_shape=jax.ShapeDtypeStruct((b, n2 * opitch, d), F32),
        scratch_shapes=[pltpu.VMEM((n1 * pitch, LANES), jnp.uint32)],
        compiler_params=_params("arbitrary", "arbitrary"),
        name="fourier_seq_dft",
    )(zp.reshape(b, n1 * pitch, d), m1, m2, twc, tws)
    return out.reshape(b * n2, opitch, d)


def _seq_dft_direct_kernel(z_ref, m_ref, o_ref, *, inv_norm):
    zr, zi = _unpack_complex(z_ref[0])
    o_ref[0] = jnp.dot(m_ref[...], jnp.concatenate([zr, zi], axis=0), preferred_element_type=F32) * inv_norm


def _seq_dft_direct(z, b, seq, d, inv_norm):
    dt = min(1024, d)
    c, s = _cos_sin(seq, seq, seq)
    m = jnp.concatenate([c, s], axis=1).astype(BF16)
    out = pl.pallas_call(
        functools.partial(_seq_dft_direct_kernel, inv_norm=inv_norm),
        grid=(b, d // dt),
        in_specs=[
            pl.BlockSpec((1, seq, dt), lambda bi, k: (bi, 0, k)),
            pl.BlockSpec((seq, 2 * seq), lambda bi, k: (0, 0)),
        ],
        out_specs=pl.BlockSpec((1, seq, dt), lambda bi, k: (bi, 0, k)),
        out_shape=jax.ShapeDtypeStruct((b, seq, d), F32),
        compiler_params=_params("arbitrary", "arbitrary"),
        name="fourier_seq_dft_direct",
    )(z.reshape(b, seq, d), m)
    return out.reshape(b * seq, d)


def _fourier_mixed(proj, b, seq, d, two_stage):
    inv_norm = 1.0 / math.sqrt(seq * (d // FOURIER_GROUPS))
    if two_stage:
        n2 = seq // DFT_N1
        pitch = n2 + SUBLANES
        z = _chan_dft(proj, b * seq, d, run=n2, pitch=pitch)
        return _seq_dft(z, b, seq, d, pitch, DFT_N1 + SUBLANES, inv_norm)
    return _seq_dft_direct(_chan_dft(proj, b * seq, d), b, seq, d, inv_norm)


def _rope_tables(n_tokens):
    rows = n_tokens // GRID_W
    row = jnp.repeat(jnp.arange(rows), GRID_W).astype(F32)
    col = jnp.tile(jnp.arange(GRID_W), rows).astype(F32)
    quarter = HEAD_DIM // 4
    inv_freq = ROPE_BASE ** (-jnp.arange(quarter, dtype=F32) / quarter)
    ang_r = row[:, None] * inv_freq[None, :]
    ang_c = col[:, None] * inv_freq[None, :]
    ang = jnp.concatenate([ang_r, ang_r, ang_c, ang_c], axis=-1)
    sign = jnp.where((jnp.arange(HEAD_DIM) % (HEAD_DIM // 2)) < quarter, -1.0, 1.0).astype(F32)
    reps = LANES // HEAD_DIM
    return jnp.tile(jnp.cos(ang), (1, reps)), jnp.tile(jnp.sin(ang) * sign, (1, reps))


def _rope128(xf, cos, sin_signed):
    quarter = HEAD_DIM // 4
    lane = lax.broadcasted_iota(jnp.int32, xf.shape, 1)
    from_hi = pltpu.roll(xf, LANES - quarter, axis=1)
    from_lo = pltpu.roll(xf, quarter, axis=1)
    partner = jnp.where((lane % (HEAD_DIM // 2)) < quarter, from_hi, from_lo)
    return xf * cos + partner * sin_signed


def _k_rope_kernel(k_ref, cos_ref, sin_ref, o_ref):
    cos = cos_ref[...]
    sin = sin_ref[...]
    for c in range(k_ref.shape[1] // LANES):
        cols = slice(c * LANES, (c + 1) * LANES)
        o_ref[:, cols] = _rope128(k_ref[:, cols].astype(F32), cos, sin).astype(BF16)


def _k_rope(proj, t, k_col, kvw, seq, cos, sin):
    tm = min(1024, seq)
    assert seq % tm == 0 and kvw % LANES == 0
    tiles_per_seq = seq // tm
    return pl.pallas_call(
        _k_rope_kernel,
        grid=(t // tm,),
        in_specs=[
            pl.BlockSpec((tm, kvw), lambda i: (i, k_col)),
            pl.BlockSpec((tm, LANES), lambda i: (i % tiles_per_seq, 0)),
            pl.BlockSpec((tm, LANES), lambda i: (i % tiles_per_seq, 0)),
        ],
        out_specs=pl.BlockSpec((tm, kvw), lambda i: (i, 0)),
        out_shape=jax.ShapeDtypeStruct((t, kvw), BF16),
        compiler_params=_params("arbitrary"),
        name="attn_k_rope",
    )(proj, cos, sin)


def _head_halves(cat_f32, lo):
    lane = lax.broadcasted_iota(jnp.int32, cat_f32.shape, 1)
    same = jnp.where((lane >= lo) & (lane < lo + HEAD_DIM), cat_f32, 0.0)
    swap = pltpu.roll(same, HEAD_DIM, axis=1)
    low, high = (same, swap) if lo == 0 else (swap, same)
    return low.astype(BF16), high.astype(BF16)


def _attn_kernel(*refs, band, bq, seq, n_ctx, groups):
    if band:
        (sink_ref, q_ref, cos_ref, sin_ref, kp_ref, kc_ref, kn_ref,
         vp_ref, vc_ref, vn_ref, kx_ref, vx_ref, o_ref, cap_scr) = refs
    else:
        sink_ref, q_ref, kx_ref, vx_ref, o_ref = refs
    n_band = bq + 2 * WINDOW if band else 0
    nk = n_band + n_ctx
    if band:
        start = pl.program_id(1) * bq
        qpos = start + lax.broadcasted_iota(jnp.int32, (bq, nk), 0)
        col = lax.broadcasted_iota(jnp.int32, (bq, nk), 1)
        kpos = start - WINDOW + col
        valid = ((jnp.abs(qpos - kpos) <= WINDOW) & (kpos >= 0) & (kpos < seq)) | (col >= n_band)
        cap_scr[...] = jnp.where(valid, jnp.inf, NEG_INF)
        cos = cos_ref[...]
        sin = sin_ref[...]
    lane_q = lax.broadcasted_iota(jnp.int32, (bq, LANES), 1)
    nt = (((1,), (1,)), ((), ()))

    def probs(s, sink2):
        if band:
            s = jnp.minimum(s, cap_scr[...])
        m = jnp.maximum(jnp.max(s, axis=-1, keepdims=True), sink2)
        p = jnp.exp2(s - m)
        denom = jnp.sum(p, axis=-1, keepdims=True) + jnp.exp2(sink2 - m)
        return p.astype(BF16), 1.0 / denom

    for h in range(N_KV_HEADS):
        pair_cols = slice((h // 2) * LANES, (h // 2 + 1) * LANES)
        k_refs = (kp_ref, kc_ref, kn_ref, kx_ref) if band else (kx_ref,)
        v_refs = (vp_ref, vc_ref, vn_ref, vx_ref) if band else (vx_ref,)
        kcat = jnp.concatenate([r[:, pair_cols].astype(F32) for r in k_refs], axis=0)
        vcat = jnp.concatenate([r[:, pair_cols].astype(F32) for r in v_refs], axis=0)
        lo = (h % 2) * HEAD_DIM
        k_lo, k_hi = _head_halves(kcat, lo)
        v_lo, v_hi = _head_halves(vcat, lo)
        for p in range(groups // 2):
            off = (h * groups + 2 * p) * HEAD_DIM
            cols = slice(off, off + LANES)
            qf = q_ref[:, cols].astype(F32)
            if band:
                qf = _rope128(qf, cos, sin)
            qb = (qf * (ATTN_SCALE * LOG2_E)).astype(BF16)
            s0 = lax.dot_general(qb, k_lo, nt, preferred_element_type=F32)
            s1 = lax.dot_general(qb, k_hi, nt, preferred_element_type=F32)
            p0, inv0 = probs(s0, sink_ref[h * groups + 2 * p] * LOG2_E)
            p1, inv1 = probs(s1, sink_ref[h * groups + 2 * p + 1] * LOG2_E)
            o_pair = (jnp.dot(p0, v_lo, preferred_element_type=F32)
                      + jnp.dot(p1, v_hi, preferred_element_type=F32))
            o_ref[:, cols] = (o_pair * jnp.where(lane_q < HEAD_DIM, inv0, inv1)).astype(BF16)


def _attention(sink, proj, q_col, v_col, k_rot, proj_c, kc_col, vc_col, b, seq, n_ctx, d, cos, sin):
    kvw = N_KV_HEADS * HEAD_DIM
    groups = d // kvw
    bq = min(256, seq)
    assert seq % bq == 0 and bq % WINDOW == 0 and groups % 2 == 0 and kvw % LANES == 0
    nq = seq // bq
    wpb = bq // WINDOW
    n_halo = b * seq // WINDOW

    def prev_map(col):
        return lambda bi, i: (jnp.maximum((bi * nq + i) * wpb - 1, 0), col)

    def next_map(col):
        return lambda bi, i: (jnp.minimum((bi * nq + i + 1) * wpb, n_halo - 1), col)

    def cur_map(col):
        return lambda bi, i: (bi * nq + i, col)

    return pl.pallas_call(
        functools.partial(_attn_kernel, band=True, bq=bq, seq=seq, n_ctx=n_ctx, groups=groups),
        grid=(b, nq),
        in_specs=[
            pl.BlockSpec(memory_space=pltpu.SMEM),
            pl.BlockSpec((bq, d), cur_map(q_col)),
            pl.BlockSpec((bq, LANES), lambda bi, i: (i, 0)),
            pl.BlockSpec((bq, LANES), lambda bi, i: (i, 0)),
            pl.BlockSpec((WINDOW, kvw), prev_map(0)),
            pl.BlockSpec((bq, kvw), cur_map(0)),
            pl.BlockSpec((WINDOW, kvw), next_map(0)),
            pl.BlockSpec((WINDOW, kvw), prev_map(v_col)),
            pl.BlockSpec((bq, kvw), cur_map(v_col)),
            pl.BlockSpec((WINDOW, kvw), next_map(v_col)),
            pl.BlockSpec((n_ctx, kvw), lambda bi, i: (bi, kc_col)),
            pl.BlockSpec((n_ctx, kvw), lambda bi, i: (bi, vc_col)),
        ],
        out_specs=pl.BlockSpec((bq, d), lambda bi, i: (bi * nq + i, 0)),
        out_shape=jax.ShapeDtypeStruct((b * seq, d), BF16),
        scratch_shapes=[pltpu.VMEM((bq, bq + 2 * WINDOW + n_ctx), F32)],
        compiler_params=_params("arbitrary", "arbitrary"),
        name="window_attention",
    )(sink, proj, cos, sin, k_rot, k_rot, k_rot, proj, proj, proj, proj_c, proj_c)


def _ctx_attention(sink, proj_c, q_col, k_col, v_col, b, n_ctx, d):
    kvw = N_KV_HEADS * HEAD_DIM
    groups = d // kvw
    return pl.pallas_call(
        functools.partial(_attn_kernel, band=False, bq=n_ctx, seq=n_ctx, n_ctx=n_ctx, groups=groups),
        grid=(b,),
        in_specs=[
            pl.BlockSpec(memory_space=pltpu.SMEM),
            pl.BlockSpec((n_ctx, d), lambda bi: (bi, q_col)),
            pl.BlockSpec((n_ctx, kvw), lambda bi: (bi, k_col)),
            pl.BlockSpec((n_ctx, kvw), lambda bi: (bi, v_col)),
        ],
        out_specs=pl.BlockSpec((n_ctx, d), lambda bi: (bi, 0)),
        out_shape=jax.ShapeDtypeStruct((b * n_ctx, d), BF16),
        compiler_params=_params("arbitrary"),
        name="context_attention",
    )(sink, proj_c, proj_c, proj_c)


def _conv_kernel(a_ref, ag_ref, ap_ref, agp_ref, an_ref, agn_ref, w_ref, b_ref, lg_ref, lb_ref,
                 o_ref, g_scr, sh_scr, c_scr, wb_scr, *, tiles_per_seq, kw):
    tm, d = a_ref.shape
    halo = CONV_HALO
    span = tm + 2 * halo
    i = pl.program_id(0)
    first = (i % tiles_per_seq) == 0
    last = (i % tiles_per_seq) == tiles_per_seq - 1

    def glu(a, ag):
        return a.astype(F32) * jax.nn.sigmoid(ag.astype(F32))

    g_scr[0:halo, :] = jnp.where(first, 0.0, glu(ap_ref[...], agp_ref[...]))

    def glu_body(r, carry):
        rows = pl.ds(pl.multiple_of(r * ROW_CHUNK, ROW_CHUNK), ROW_CHUNK)
        g_scr[pl.ds(pl.multiple_of(halo + r * ROW_CHUNK, ROW_CHUNK), ROW_CHUNK), :] = glu(a_ref[rows, :], ag_ref[rows, :])
        return carry

    lax.fori_loop(0, tm // ROW_CHUNK, glu_body, 0, unroll=ROW_UNROLL)
    g_scr[halo + tm:span, :] = jnp.where(last, 0.0, glu(an_ref[...], agn_ref[...]))
    g_scr[span:span + SUBLANES, :] = jnp.zeros((SUBLANES, d), F32)

    for r in range(1, SUBLANES):
        sh_scr[r - 1] = g_scr[r:r + span, :]

    def shifted(r, rows, cols):
        return g_scr[rows, cols] if r == 0 else sh_scr[r - 1, rows, cols]

    pad = kw // 2
    lw = 4 * LANES
    bias = b_ref[...]
    lg = lg_ref[...]
    lb = lb_ref[...]

    @pl.when(i == 0)
    def _():
        for j in range(kw):
            wb_scr[j] = jnp.broadcast_to(w_ref[j:j + 1, :], (SUBLANES, d))

    groups = ROW_CHUNK // SUBLANES

    def row_body(rc, carry):
        base = pl.multiple_of(rc * ROW_CHUNK, ROW_CHUNK)
        for ch in range(d // lw):
            cols = slice(ch * lw, (ch + 1) * lw)
            acc = [jnp.zeros((SUBLANES, lw), F32) for _ in range(groups)]
            for j in range(kw):
                off = j + halo - pad
                w = wb_scr[j, :, cols]
                for s in range(groups):
                    rows = pl.ds(pl.multiple_of(base + (off // SUBLANES + s) * SUBLANES, SUBLANES), SUBLANES)
                    acc[s] = acc[s] + w * shifted(off % SUBLANES, rows, cols)
            for s in range(groups):
                rows = pl.ds(pl.multiple_of(base + s * SUBLANES, SUBLANES), SUBLANES)
                c_scr[rows, cols] = acc[s] + bias[:, cols]
        y = c_scr[pl.ds(base, ROW_CHUNK), :]
        mu = jnp.mean(y, axis=-1, keepdims=True)
        var = jnp.mean(jnp.square(y - mu), axis=-1, keepdims=True)
        yn = (y - mu) * lax.rsqrt(var + NORM_EPS) * lg + lb
        o_ref[pl.ds(base, ROW_CHUNK), :] = _silu(yn).astype(BF16)
        return carry

    lax.fori_loop(0, tm // ROW_CHUNK, row_body, 0, unroll=2)


def _conv_module(proj, t, seq, d, dw_w, dw_b, ln_g, ln_b):
    kw = dw_w.shape[0]
    tm = min(256, seq)
    halo = CONV_HALO
    assert seq % tm == 0 and tm % halo == 0 and kw // 2 <= halo and d % (4 * LANES) == 0
    tiles_per_seq = seq // tm
    hpt = tm // halo
    n_halo = t // halo
    span = tm + 2 * halo

    def prev_map(col):
        return lambda i: (jnp.maximum(i * hpt - 1, 0), col)

    def next_map(col):
        return lambda i: (jnp.minimum((i + 1) * hpt, n_halo - 1), col)

    vec = lambda: pl.BlockSpec((1, d), lambda i: (0, 0))
    return pl.pallas_call(
        functools.partial(_conv_kernel, tiles_per_seq=tiles_per_seq, kw=kw),
        grid=(t // tm,),
        in_specs=[
            pl.BlockSpec((tm, d), lambda i: (i, 0)),
            pl.BlockSpec((tm, d), lambda i: (i, 1)),
            pl.BlockSpec((halo, d), prev_map(0)),
            pl.BlockSpec((halo, d), prev_map(1)),
            pl.BlockSpec((halo, d), next_map(0)),
            pl.BlockSpec((halo, d), next_map(1)),
            pl.BlockSpec((kw, d), lambda i: (0, 0)),
            vec(), vec(), vec(),
        ],
        out_specs=pl.BlockSpec((tm, d), lambda i: (i, 0)),
        out_shape=jax.ShapeDtypeStruct((t, d), BF16),
        scratch_shapes=[
            pltpu.VMEM((span + SUBLANES, d), F32),
            pltpu.VMEM((SUBLANES - 1, span, d), F32),
            pltpu.VMEM((tm, d), F32),
            pltpu.VMEM((kw, SUBLANES, d), F32),
        ],
        compiler_params=_params("arbitrary"),
        name="conformer_conv",
    )(proj, proj, proj, proj, proj, proj, dw_w, dw_b.reshape(1, d), ln_g.reshape(1, d), ln_b.reshape(1, d))


def kernel(x, c, ctx, c_ctx, norm_g, ada_w, ada_b, four_w_in, four_w_out, attn_w_in, attn_sink,
           attn_w_out, conv_w_in, conv_dw_w, conv_dw_b, conv_ln_g, conv_ln_b, conv_w_out, final_g):
    b, seq, d = x.shape
    n_ctx = ctx.shape[1]
    depth = norm_g.shape[0]
    kvw = N_KV_HEADS * HEAD_DIM
    assert d % kvw == 0 and seq % GRID_W == 0

    n_rows = -(-(b + 1) // SUBLANES) * SUBLANES
    cond = jnp.concatenate([c, c_ctx[None, :], jnp.zeros((n_rows - b - 1, d), F32)], axis=0)
    mod = _modulation(cond, ada_w, ada_b)

    cos, sin = _rope_tables(seq)
    qe = d
    ke = qe + kvw
    ve = ke + kvw
    attn_w_perm = jnp.concatenate([attn_w_in[:, :, ve:], attn_w_in[:, :, :ve]], axis=-1)

    x2 = x.reshape(b * seq, d)
    c2 = ctx.reshape(b * n_ctx, d)
    for i in range(depth):
        kind, j = i % N_MIXERS, i // N_MIXERS
        with_ctx = i < depth - 1
        last = i == depth - 1
        shift = mod[i, :, 0:d].reshape(n_rows, 1, d)
        scale = mod[i, :, d:2 * d].reshape(n_rows, 1, d)
        gate = mod[i, :, 2 * d:3 * d].reshape(n_rows, 1, d)
        if kind == 0:
            w_in = four_w_in[j].astype(BF16)
            w_out = four_w_out[j].astype(BF16)
            proj = _in_proj(x2, norm_g[i], shift, scale, w_in, seq, 0)
            a = _fourier_mixed(proj, b, seq, d, two_stage=True)
            z_col = 1
            if with_ctx:
                proj_c = _in_proj(c2, norm_g[i], shift, scale, w_in, b * n_ctx, b)
                a_c = _fourier_mixed(proj_c, b, n_ctx, d, two_stage=False)
        elif kind == 1:
            w_in = attn_w_perm[j].astype(BF16)
            w_out = attn_w_out[j].astype(BF16)
            z_col, q_col = 0, 1
            k_col = 2 * d // kvw
            v_col = k_col + 1
            proj = _in_proj(x2, norm_g[i], shift, scale, w_in, seq, 0)
            proj_c = _in_proj(c2, norm_g[i], shift, scale, w_in, b * n_ctx, b)
            k_rot = _k_rope(proj, b * seq, k_col, kvw, seq, cos, sin)
            a = _attention(attn_sink[j], proj, q_col, v_col, k_rot, proj_c, k_col, v_col,
                           b, seq, n_ctx, d, cos, sin)
            if with_ctx:
                a_c = _ctx_attention(attn_sink[j], proj_c, q_col, k_col, v_col, b, n_ctx, d)
        else:
            w_in = conv_w_in[j].astype(BF16)
            w_out = conv_w_out[j].astype(BF16)
            z_col = 2
            conv_args = (conv_dw_w[j], conv_dw_b[j], conv_ln_g[j], conv_ln_b[j])
            proj = _in_proj(x2, norm_g[i], shift, scale, w_in, seq, 0)
            a = _conv_module(proj, b * seq, seq, d, *conv_args)
            if with_ctx:
                proj_c = _in_proj(c2, norm_g[i], shift, scale, w_in, b * n_ctx, b)
                a_c = _conv_module(proj_c, b * n_ctx, n_ctx, d, *conv_args)
        x2 = _out_proj(a, proj, z_col, x2, gate, w_out, final_g, seq, 0, final_norm=last)
        if with_ctx:
            c2 = _out_proj(a_c, proj_c, z_col, c2, gate, w_out, final_g, b * n_ctx, b, final_norm=False)
    return x2.reshape(b, seq, d)
```

```python
import functools
import math

import jax
import jax.numpy as jnp
from jax import lax
from jax.experimental import pallas as pl
from jax.experimental.pallas import tpu as pltpu

HEAD_DIM = 64
N_KV_HEADS = 4
FOURIER_GROUPS = 8
WINDOW = 128
GRID_W = 64
N_MIXERS = 3
NORM_EPS = 1e-6
ROPE_BASE = 10000.0
NEG_INF = -1e30
ATTN_SCALE = HEAD_DIM ** -0.5
LOG2_E = math.log2(math.e)

LANES = 128
SUBLANES = 8
MXU_COLS = 256
VMEM_LIMIT_BYTES = 56 * 1024 * 1024

DFT_N1 = 128
IN_PROJ_COL_TILES = 4
INTERLEAVE = 4
ROW_CHUNK = 16
ROW_UNROLL = 4
CONV_HALO = 16

F32 = jnp.float32
BF16 = jnp.bfloat16


def _params(*sem):
    return pltpu.CompilerParams(dimension_semantics=sem, vmem_limit_bytes=VMEM_LIMIT_BYTES)


def _silu(v):
    return v * jax.nn.sigmoid(v)


def _col_pieces(n):
    units = -(-n // MXU_COLS)
    p = min(INTERLEAVE, units)
    bounds = [min(n, (k * units // p) * MXU_COLS) for k in range(p + 1)]
    return list(zip(bounds[:-1], bounds[1:]))


def _pick_tile(n, cap, quantum):
    if n <= cap:
        return n
    t = cap - cap % quantum
    while n % t:
        t -= quantum
    return t


def _mod_kernel(cond_ref, w_ref, b_ref, o_ref):
    s = _silu(cond_ref[...]).astype(BF16)
    o_ref[0] = jnp.dot(s, w_ref[0].astype(BF16), preferred_element_type=F32) + b_ref[0]


def _modulation(cond, ada_w, ada_b):
    depth, d, d3 = ada_w.shape
    r = cond.shape[0]
    tn = _pick_tile(d3, 1024, 2 * LANES)
    return pl.pallas_call(
        _mod_kernel,
        grid=(depth, d3 // tn),
        in_specs=[
            pl.BlockSpec((r, d), lambda l, j: (0, 0)),
            pl.BlockSpec((1, d, tn), lambda l, j: (l, 0, j)),
            pl.BlockSpec((1, 1, tn), lambda l, j: (l, 0, j)),
        ],
        out_specs=pl.BlockSpec((1, r, tn), lambda l, j: (l, 0, j)),
        out_shape=jax.ShapeDtypeStruct((depth, r, d3), F32),
        compiler_params=_params("arbitrary", "arbitrary"),
        name="ada_modulation",
    )(cond, ada_w, ada_b.reshape(depth, 1, d3))


def _in_proj_kernel(x_ref, g_ref, shift_ref, scale_ref, w_ref, o_ref, h0_scr, h1_scr):
    i = pl.program_id(0)
    j = pl.program_id(1)
    r = x_ref.shape[0]

    @pl.when((i == 0) & (j == 0))
    def _():
        h1_scr[...] = jnp.zeros(h1_scr.shape, BF16)

    def step(h_write, h_read):
        g = g_ref[...]
        sh = shift_ref[0]
        sc = 1.0 + scale_ref[0]
        tn = w_ref.shape[1]
        n_chunks = r // ROW_CHUNK
        pieces = _col_pieces(tn)
        for piece, (lo, hi) in enumerate(pieces):
            cols = slice(lo, hi)
            o_ref[:, cols] = jnp.dot(h_read[...], w_ref[:, cols], preferred_element_type=F32).astype(BF16)
            for c in range(piece * n_chunks // len(pieces), (piece + 1) * n_chunks // len(pieces)):
                xf = x_ref[c * ROW_CHUNK:(c + 1) * ROW_CHUNK, :]
                ms = jnp.mean(xf * xf, axis=-1, keepdims=True)
                y = xf * lax.rsqrt(ms + NORM_EPS)
                rows = pl.ds(pl.multiple_of(j * r + c * ROW_CHUNK, ROW_CHUNK), ROW_CHUNK)
                h_write[rows, :] = ((y * g) * sc + sh).astype(BF16)

    @pl.when(i % 2 == 0)
    def _():
        step(h0_scr, h1_scr)

    @pl.when(i % 2 == 1)
    def _():
        step(h1_scr, h0_scr)


def _in_proj(x2, norm_g, shift, scale, w_bf16, rows_per_mod, mod_row0):
    t, d = x2.shape
    n_out = w_bf16.shape[1]
    tm = min(1024, rows_per_mod, t)
    nn = IN_PROJ_COL_TILES
    tn = n_out // nn
    r = tm // nn
    assert t % tm == 0 and rows_per_mod % tm == 0 and n_out % nn == 0 and tn % LANES == 0 and r % ROW_CHUNK == 0
    nm = t // tm
    tiles_per_mod = rows_per_mod // tm
    norm_tile = lambda i: jnp.minimum(i, nm - 1)
    mod_map = lambda i, j: (mod_row0 + norm_tile(i) // tiles_per_mod, 0, 0)
    return pl.pallas_call(
        _in_proj_kernel,
        grid=(nm + 1, nn),
        in_specs=[
            pl.BlockSpec((r, d), lambda i, j: (norm_tile(i) * nn + j, 0)),
            pl.BlockSpec((1, d), lambda i, j: (0, 0)),
            pl.BlockSpec((1, 1, d), mod_map),
            pl.BlockSpec((1, 1, d), mod_map),
            pl.BlockSpec((d, tn), lambda i, j: (0, j)),
        ],
        out_specs=pl.BlockSpec((tm, tn), lambda i, j: (jnp.where(i == 0, nm, i - 1), j)),
        out_shape=jax.ShapeDtypeStruct((t + tm, n_out), BF16),
        scratch_shapes=[pltpu.VMEM((tm, d), BF16), pltpu.VMEM((tm, d), BF16)],
        compiler_params=_params("arbitrary", "arbitrary"),
        name="norm_in_proj",
    )(x2, norm_g.reshape(1, d), shift, scale, w_bf16)


def _out_proj_kernel(a_ref, z_ref, x_ref, gate_ref, w_ref, fg_ref, o_ref, y0_scr, y1_scr, *, final_norm, a_run):
    i = pl.program_id(0)
    tm = x_ref.shape[0]

    @pl.when(i == 0)
    def _():
        y1_scr[...] = jnp.zeros(y1_scr.shape, BF16)

    def step(y_write, y_read):
        d = w_ref.shape[1]
        n_chunks = tm // ROW_CHUNK
        gate = gate_ref[0]
        pieces = _col_pieces(d)
        for piece, (lo, hi) in enumerate(pieces):
            cols = slice(lo, hi)
            o = jnp.dot(y_read[...], w_ref[:, cols], preferred_element_type=F32)
            o_ref[:, cols] = x_ref[:, cols] + gate[:, cols] * o
            for c in range(piece * n_chunks // len(pieces), (piece + 1) * n_chunks // len(pieces)):
                rows = slice(c * ROW_CHUNK, (c + 1) * ROW_CHUNK)
                if len(a_ref.shape) == 3:
                    off = (c * ROW_CHUNK) % a_run
                    a = a_ref[(c * ROW_CHUNK) // a_run, off:off + ROW_CHUNK, :]
                else:
                    a = a_ref[rows, :]
                y_write[rows, :] = (a.astype(F32) * _silu(z_ref[rows, :].astype(F32))).astype(BF16)
        if final_norm:
            fg = fg_ref[...]
            for c in range(n_chunks):
                rows = slice(c * ROW_CHUNK, (c + 1) * ROW_CHUNK)
                xn = o_ref[rows, :]
                ms = jnp.mean(xn * xn, axis=-1, keepdims=True)
                o_ref[rows, :] = (xn * lax.rsqrt(ms + NORM_EPS)) * fg

    @pl.when(i % 2 == 0)
    def _():
        step(y0_scr, y1_scr)

    @pl.when(i % 2 == 1)
    def _():
        step(y1_scr, y0_scr)


def _out_proj(a, z_arr, z_col, x2, gate, w_bf16, final_g, rows_per_mod, mod_row0, final_norm):
    t, d = x2.shape
    tm = min(512, rows_per_mod, t)
    assert t % tm == 0 and rows_per_mod % tm == 0
    nm = t // tm
    tiles_per_mod = rows_per_mod // tm
    gate_tile = lambda i: jnp.minimum(i, nm - 1)
    mm_tile = lambda i: jnp.maximum(i - 1, 0)
    a_run = None
    a_spec = pl.BlockSpec((tm, d), lambda i: (gate_tile(i), 0))
    if a.ndim == 3:
        a_run = t // a.shape[0]
        assert tm % a_run == 0 and a_run % ROW_CHUNK == 0
        a_spec = pl.BlockSpec((tm // a_run, a.shape[1], d), lambda i: (gate_tile(i), 0, 0))
    return pl.pallas_call(
        functools.partial(_out_proj_kernel, final_norm=final_norm, a_run=a_run),
        grid=(nm + 1,),
        in_specs=[
            a_spec,
            pl.BlockSpec((tm, d), lambda i: (gate_tile(i), z_col)),
            pl.BlockSpec((tm, d), lambda i: (mm_tile(i), 0)),
            pl.BlockSpec((1, 1, d), lambda i: (mod_row0 + mm_tile(i) // tiles_per_mod, 0, 0)),
            pl.BlockSpec((d, d), lambda i: (0, 0), pipeline_mode=pl.Buffered(1)),
            pl.BlockSpec((1, d), lambda i: (0, 0)),
        ],
        out_specs=pl.BlockSpec((tm, d), lambda i: (mm_tile(i), 0)),
        out_shape=jax.ShapeDtypeStruct((t, d), F32),
        scratch_shapes=[pltpu.VMEM((tm, d), BF16), pltpu.VMEM((tm, d), BF16)],
        compiler_params=_params("arbitrary"),
        name="gated_out_proj",
    )(a, z_arr, x2, gate, w_bf16, final_g.reshape(1, d))


def _cos_sin(n_rows, n_cols, period):
    r = jnp.arange(n_rows, dtype=jnp.int32)[:, None]
    c = jnp.arange(n_cols, dtype=jnp.int32)[None, :]
    ang = ((r * c) % period).astype(F32) * (2.0 * math.pi / period)
    return jnp.cos(ang), jnp.sin(ang)


def _pack_complex(re, im):
    hi = lax.bitcast_convert_type(re.astype(BF16).astype(F32), jnp.uint32)
    lo = lax.bitcast_convert_type(im.astype(BF16).astype(F32), jnp.uint32)
    return hi | (lo >> 16)


def _unpack_complex(w):
    re = lax.bitcast_convert_type(w & jnp.uint32(0xFFFF0000), F32)
    im = lax.bitcast_convert_type(w << 16, F32)
    return re.astype(BF16), im.astype(BF16)


def _chan_dft_kernel(u_ref, m_ref, z_ref, *, groups, run):
    gs = m_ref.shape[0]
    tm = u_ref.shape[0]
    for g in range(groups):
        cols = slice(g * gs, (g + 1) * gs)
        r = jnp.dot(u_ref[:, cols], m_ref[...], preferred_element_type=F32)
        packed = _pack_complex(r[:, :gs], r[:, gs:])
        if run is None:
            z_ref[:, cols] = packed
        else:
            for k in range(tm // run):
                z_ref[k, 0:run, cols] = packed[k * run:(k + 1) * run]
    if run is not None:
        pad = z_ref.shape[1] - run
        z_ref[:, run:, :] = jnp.zeros((tm // run, pad, z_ref.shape[2]), jnp.uint32)


def _chan_dft(proj, t, d, run=None, pitch=None):
    gs = d // FOURIER_GROUPS
    c, s = _cos_sin(gs, gs, gs)
    m = jnp.concatenate([c, -s], axis=1).astype(BF16)
    tm = min(1024, t)
    assert t % tm == 0
    if run is None:
        out_spec = pl.BlockSpec((tm, d), lambda i: (i, 0))
        out_shape = jax.ShapeDtypeStruct((t, d), jnp.uint32)
    else:
        assert tm % run == 0
        out_spec = pl.BlockSpec((tm // run, pitch, d), lambda i: (i, 0, 0))
        out_shape = jax.ShapeDtypeStruct((t // run, pitch, d), jnp.uint32)
    return pl.pallas_call(
        functools.partial(_chan_dft_kernel, groups=FOURIER_GROUPS, run=run),
        grid=(t // tm,),
        in_specs=[
            pl.BlockSpec((tm, d), lambda i: (i, 0)),
            pl.BlockSpec((gs, 2 * gs), lambda i: (0, 0)),
        ],
        out_specs=out_spec,
        out_shape=out_shape,
        compiler_params=_params("arbitrary"),
        name="fourier_chan_dft",
    )(proj, m)


def _stack_pair(parts):
    return jnp.concatenate([jnp.concatenate([parts[0][0], parts[1][0]], axis=1),
                            jnp.concatenate([parts[0][1], parts[1][1]], axis=1)], axis=0)


def _seq_dft_kernel(z_ref, m1_ref, m2_ref, twc_ref, tws_ref, o_ref, y_scr, *, n1, n2, pitch, opitch, inv_norm):
    twc = twc_ref[...]
    tws = tws_ref[...]
    for q in range(0, n2, 2):
        zin = _stack_pair([_unpack_complex(z_ref[0, pl.ds(qq, n1, stride=pitch), :]) for qq in (q, q + 1)])
        y = jnp.dot(m1_ref[...], zin, preferred_element_type=F32)
        for t, qq in enumerate((q, q + 1)):
            yr = y[:n1, t * LANES:(t + 1) * LANES]
            yi = y[n1:, t * LANES:(t + 1) * LANES]
            c = twc[:, qq:qq + 1]
            s = tws[:, qq:qq + 1]
            y_scr[pl.ds(qq, n1, stride=pitch), :] = _pack_complex(yr * c + yi * s, yi * c - yr * s)
    for k in range(0, n1, 2):
        yin = _stack_pair([_unpack_complex(y_scr[kk * pitch:kk * pitch + n2, :]) for kk in (k, k + 1)])
        r = jnp.dot(m2_ref[...], yin, preferred_element_type=F32) * inv_norm
        for t, kk in enumerate((k, k + 1)):
            o_ref[0, pl.ds(kk, n2, stride=opitch), :] = r[:, t * LANES:(t + 1) * LANES]
    for j in range(n1, opitch):
        o_ref[0, pl.ds(j, n2, stride=opitch), :] = jnp.zeros((n2, LANES), F32)


def _seq_dft(zp, b, seq, d, pitch, opitch, inv_norm):
    n1 = DFT_N1
    n2 = seq // n1
    assert seq % n1 == 0 and n1 % 2 == 0 and n2 % 2 == 0 and d % LANES == 0
    c1, s1 = _cos_sin(n1, n1, n1)
    m1 = jnp.concatenate([jnp.concatenate([c1, s1], axis=1),
                          jnp.concatenate([-s1, c1], axis=1)], axis=0).astype(BF16)
    c2, s2 = _cos_sin(n2, n2, n2)
    m2 = jnp.concatenate([c2, s2], axis=1).astype(BF16)
    twc, tws = _cos_sin(n1, n2, seq)
    const = lambda shape: pl.BlockSpec(shape, lambda bi, k: (0, 0))
    out = pl.pallas_call(
        functools.partial(_seq_dft_kernel, n1=n1, n2=n2, pitch=pitch, opitch=opitch, inv_norm=inv_norm),
        grid=(b, d // LANES),
        in_specs=[
            pl.BlockSpec((1, n1 * pitch, LANES), lambda bi, k: (bi, 0, k)),
            const((2 * n1, 2 * n1)), const((n2, 2 * n2)), const((n1, n2)), const((n1, n2)),
        ],
        out_specs=pl.BlockSpec((1, n2 * opitch, LANES), lambda bi, k: (bi, 0, k)),
        out_shape=jax.ShapeDtypeStruct((b, n2 * opitch, d), F32),
        scratch_shapes=[pltpu.VMEM((n1 * pitch, LANES), jnp.uint32)],
        compiler_params=_params("arbitrary", "arbitrary"),
        name="fourier_seq_dft",
    )(zp.reshape(b, n1 * pitch, d), m1, m2, twc, tws)
    return out.reshape(b * n2, opitch, d)


def _seq_dft_direct_kernel(z_ref, m_ref, o_ref, *, inv_norm):
    zr, zi = _unpack_complex(z_ref[0])
    o_ref[0] = jnp.dot(m_ref[...], jnp.concatenate([zr, zi], axis=0), preferred_element_type=F32) * inv_norm


def _seq_dft_direct(z, b, seq, d, inv_norm):
    dt = min(1024, d)
    c, s = _cos_sin(seq, seq, seq)
    m = jnp.concatenate([c, s], axis=1).astype(BF16)
    out = pl.pallas_call(
        functools.partial(_seq_dft_direct_kernel, inv_norm=inv_norm),
        grid=(b, d // dt),
        in_specs=[
            pl.BlockSpec((1, seq, dt), lambda bi, k: (bi, 0, k)),
            pl.BlockSpec((seq, 2 * seq), lambda bi, k: (0, 0)),
        ],
        out_specs=pl.BlockSpec((1, seq, dt), lambda bi, k: (bi, 0, k)),
        out_shape=jax.ShapeDtypeStruct((b, seq, d), F32),
        compiler_params=_params("arbitrary", "arbitrary"),
        name="fourier_seq_dft_direct",
    )(z.reshape(b, seq, d), m)
    return out.reshape(b * seq, d)


def _fourier_mixed(proj, b, seq, d, two_stage):
    inv_norm = 1.0 / math.sqrt(seq * (d // FOURIER_GROUPS))
    if two_stage:
        n2 = seq // DFT_N1
        pitch = n2 + SUBLANES
        z = _chan_dft(proj, b * seq, d, run=n2, pitch=pitch)
        return _seq_dft(z, b, seq, d, pitch, DFT_N1 + SUBLANES, inv_norm)
    return _seq_dft_direct(_chan_dft(proj, b * seq, d), b, seq, d, inv_norm)


def _rope_tables(n_tokens):
    rows = n_tokens // GRID_W
    row = jnp.repeat(jnp.arange(rows), GRID_W).astype(F32)
    col = jnp.tile(jnp.arange(GRID_W), rows).astype(F32)
    quarter = HEAD_DIM // 4
    inv_freq = ROPE_BASE ** (-jnp.arange(quarter, dtype=F32) / quarter)
    ang_r = row[:, None] * inv_freq[None, :]
    ang_c = col[:, None] * inv_freq[None, :]
    ang = jnp.concatenate([ang_r, ang_r, ang_c, ang_c], axis=-1)
    sign = jnp.where((jnp.arange(HEAD_DIM) % (HEAD_DIM // 2)) < quarter, -1.0, 1.0).astype(F32)
    reps = LANES // HEAD_DIM
    return jnp.tile(jnp.cos(ang), (1, reps)), jnp.tile(jnp.sin(ang) * sign, (1, reps))


def _rope128(xf, cos, sin_signed):
    quarter = HEAD_DIM // 4
    lane = lax.broadcasted_iota(jnp.int32, xf.shape, 1)
    from_hi = pltpu.roll(xf, LANES - quarter, axis=1)
    from_lo = pltpu.roll(xf, quarter, axis=1)
    partner = jnp.where((lane % (HEAD_DIM // 2)) < quarter, from_hi, from_lo)
    return xf * cos + partner * sin_signed


def _k_rope_kernel(k_ref, cos_ref, sin_ref, o_ref):
    cos = cos_ref[...]
    sin = sin_ref[...]
    for c in range(k_ref.shape[1] // LANES):
        cols = slice(c * LANES, (c + 1) * LANES)
        o_ref[:, cols] = _rope128(k_ref[:, cols].astype(F32), cos, sin).astype(BF16)


def _k_rope(proj, t, k_col, kvw, seq, cos, sin):
    tm = min(1024, seq)
    assert seq % tm == 0 and kvw % LANES == 0
    tiles_per_seq = seq // tm
    return pl.pallas_call(
        _k_rope_kernel,
        grid=(t // tm,),
        in_specs=[
            pl.BlockSpec((tm, kvw), lambda i: (i, k_col)),
            pl.BlockSpec((tm, LANES), lambda i: (i % tiles_per_seq, 0)),
            pl.BlockSpec((tm, LANES), lambda i: (i % tiles_per_seq, 0)),
        ],
        out_specs=pl.BlockSpec((tm, kvw), lambda i: (i, 0)),
        out_shape=jax.ShapeDtypeStruct((t, kvw), BF16),
        compiler_params=_params("arbitrary"),
        name="attn_k_rope",
    )(proj, cos, sin)


def _head_halves(cat_f32, lo):
    lane = lax.broadcasted_iota(jnp.int32, cat_f32.shape, 1)
    same = jnp.where((lane >= lo) & (lane < lo + HEAD_DIM), cat_f32, 0.0)
    swap = pltpu.roll(same, HEAD_DIM, axis=1)
    low, high = (same, swap) if lo == 0 else (swap, same)
    return low.astype(BF16), high.astype(BF16)


def _attn_kernel(*refs, band, bq, seq, n_ctx, groups):
    if band:
        (sink_ref, q_ref, cos_ref, sin_ref, kp_ref, kc_ref, kn_ref,
         vp_ref, vc_ref, vn_ref, kx_ref, vx_ref, o_ref, cap_scr) = refs
    else:
        sink_ref, q_ref, kx_ref, vx_ref, o_ref = refs
    n_band = bq + 2 * WINDOW if band else 0
    nk = n_band + n_ctx
    if band:
        start = pl.program_id(1) * bq
        qpos = start + lax.broadcasted_iota(jnp.int32, (bq, nk), 0)
        col = lax.broadcasted_iota(jnp.int32, (bq, nk), 1)
        kpos = start - WINDOW + col
        valid = ((jnp.abs(qpos - kpos) <= WINDOW) & (kpos >= 0) & (kpos < seq)) | (col >= n_band)
        cap_scr[...] = jnp.where(valid, jnp.inf, NEG_INF)
        cos = cos_ref[...]
        sin = sin_ref[...]
    lane_q = lax.broadcasted_iota(jnp.int32, (bq, LANES), 1)
    nt = (((1,), (1,)), ((), ()))

    def probs(s, sink2):
        if band:
            s = jnp.minimum(s, cap_scr[...])
        m = jnp.maximum(jnp.max(s, axis=-1, keepdims=True), sink2)
        p = jnp.exp2(s - m)
        denom = jnp.sum(p, axis=-1, keepdims=True) + jnp.exp2(sink2 - m)
        return p.astype(BF16), 1.0 / denom

    def kv_halves(h):
        pair_cols = slice((h // 2) * LANES, (h // 2 + 1) * LANES)
        k_refs = (kp_ref, kc_ref, kn_ref, kx_ref) if band else (kx_ref,)
        v_refs = (vp_ref, vc_ref, vn_ref, vx_ref) if band else (vx_ref,)
        kcat = jnp.concatenate([r[:, pair_cols].astype(F32) for r in k_refs], axis=0)
        vcat = jnp.concatenate([r[:, pair_cols].astype(F32) for r in v_refs], axis=0)
        lo = (h % 2) * HEAD_DIM
        return _head_halves(kcat, lo) + _head_halves(vcat, lo)

    def scores(h, p, kv):
        off = (h * groups + 2 * p) * HEAD_DIM
        qf = q_ref[:, off:off + LANES].astype(F32)
        if band:
            qf = _rope128(qf, cos, sin)
        qb = (qf * (ATTN_SCALE * LOG2_E)).astype(BF16)
        return (lax.dot_general(qb, kv[0], nt, preferred_element_type=F32),
                lax.dot_general(qb, kv[1], nt, preferred_element_type=F32))

    def finish(h, p, kv, s0, s1):
        off = (h * groups + 2 * p) * HEAD_DIM
        p0, inv0 = probs(s0, sink_ref[h * groups + 2 * p] * LOG2_E)
        p1, inv1 = probs(s1, sink_ref[h * groups + 2 * p + 1] * LOG2_E)
        o_pair = (jnp.dot(p0, kv[2], preferred_element_type=F32)
                  + jnp.dot(p1, kv[3], preferred_element_type=F32))
        o_ref[:, off:off + LANES] = (o_pair * jnp.where(lane_q < HEAD_DIM, inv0, inv1)).astype(BF16)

    pairs = [(h, p) for h in range(N_KV_HEADS) for p in range(groups // 2)]
    kv = {0: kv_halves(0)}
    s_cur = scores(*pairs[0], kv[0])
    for idx, (h, p) in enumerate(pairs):
        s_next = None
        if idx + 1 < len(pairs):
            hn, pn = pairs[idx + 1]
            if hn not in kv:
                kv[hn] = kv_halves(hn)
            s_next = scores(hn, pn, kv[hn])
        finish(h, p, kv[h], *s_cur)
        s_cur = s_next


def _attention(sink, proj, q_col, v_col, k_rot, proj_c, kc_col, vc_col, b, seq, n_ctx, d, cos, sin):
    kvw = N_KV_HEADS * HEAD_DIM
    groups = d // kvw
    bq = min(256, seq)
    assert seq % bq == 0 and bq % WINDOW == 0 and n_ctx % WINDOW == 0 and groups % 2 == 0 and kvw % LANES == 0
    nq = seq // bq
    wpb = bq // WINDOW
    n_halo = b * seq // WINDOW

    def prev_map(col):
        return lambda bi, i: (jnp.maximum((bi * nq + i) * wpb - 1, 0), col)

    def next_map(col):
        return lambda bi, i: (jnp.minimum((bi * nq + i + 1) * wpb, n_halo - 1), col)

    def cur_map(col):
        return lambda bi, i: (bi * nq + i, col)

    return pl.pallas_call(
        functools.partial(_attn_kernel, band=True, bq=bq, seq=seq, n_ctx=n_ctx, groups=groups),
        grid=(b, nq),
        in_specs=[
            pl.BlockSpec(memory_space=pltpu.SMEM),
            pl.BlockSpec((bq, d), cur_map(q_col)),
            pl.BlockSpec((bq, LANES), lambda bi, i: (i, 0)),
            pl.BlockSpec((bq, LANES), lambda bi, i: (i, 0)),
            pl.BlockSpec((WINDOW, kvw), prev_map(0)),
            pl.BlockSpec((bq, kvw), cur_map(0)),
            pl.BlockSpec((WINDOW, kvw), next_map(0)),
            pl.BlockSpec((WINDOW, kvw), prev_map(v_col)),
            pl.BlockSpec((bq, kvw), cur_map(v_col)),
            pl.BlockSpec((WINDOW, kvw), next_map(v_col)),
            pl.BlockSpec((n_ctx, kvw), lambda bi, i: (bi, kc_col)),
            pl.BlockSpec((n_ctx, kvw), lambda bi, i: (bi, vc_col)),
        ],
        out_specs=pl.BlockSpec((bq, d), lambda bi, i: (bi * nq + i, 0)),
        out_shape=jax.ShapeDtypeStruct((b * seq, d), BF16),
        scratch_shapes=[pltpu.VMEM((bq, bq + 2 * WINDOW + n_ctx), F32)],
        compiler_params=_params("arbitrary", "arbitrary"),
        name="window_attention",
    )(sink, proj, cos, sin, k_rot, k_rot, k_rot, proj, proj, proj, proj_c, proj_c)


def _ctx_attention(sink, proj_c, q_col, k_col, v_col, b, n_ctx, d):
    kvw = N_KV_HEADS * HEAD_DIM
    groups = d // kvw
    return pl.pallas_call(
        functools.partial(_attn_kernel, band=False, bq=n_ctx, seq=n_ctx, n_ctx=n_ctx, groups=groups),
        grid=(b,),
        in_specs=[
            pl.BlockSpec(memory_space=pltpu.SMEM),
            pl.BlockSpec((n_ctx, d), lambda bi: (bi, q_col)),
            pl.BlockSpec((n_ctx, kvw), lambda bi: (bi, k_col)),
            pl.BlockSpec((n_ctx, kvw), lambda bi: (bi, v_col)),
        ],
        out_specs=pl.BlockSpec((n_ctx, d), lambda bi: (bi, 0)),
        out_shape=jax.ShapeDtypeStruct((b * n_ctx, d), BF16),
        compiler_params=_params("arbitrary"),
        name="context_attention",
    )(sink, proj_c, proj_c, proj_c)


def _conv_kernel(a_ref, ag_ref, ap_ref, agp_ref, an_ref, agn_ref, w_ref, b_ref, lg_ref, lb_ref,
                 o_ref, g_scr, sh_scr, c_scr, wb_scr, *, tiles_per_seq, kw):
    tm, d = a_ref.shape
    halo = CONV_HALO
    span = tm + 2 * halo
    i = pl.program_id(0)
    first = (i % tiles_per_seq) == 0
    last = (i % tiles_per_seq) == tiles_per_seq - 1

    def glu(a, ag):
        return a.astype(F32) * jax.nn.sigmoid(ag.astype(F32))

    g_scr[0:halo, :] = jnp.where(first, 0.0, glu(ap_ref[...], agp_ref[...]))

    def glu_body(r, carry):
        rows = pl.ds(pl.multiple_of(r * ROW_CHUNK, ROW_CHUNK), ROW_CHUNK)
        g_scr[pl.ds(pl.multiple_of(halo + r * ROW_CHUNK, ROW_CHUNK), ROW_CHUNK), :] = glu(a_ref[rows, :], ag_ref[rows, :])
        return carry

    lax.fori_loop(0, tm // ROW_CHUNK, glu_body, 0, unroll=ROW_UNROLL)
    g_scr[halo + tm:span, :] = jnp.where(last, 0.0, glu(an_ref[...], agn_ref[...]))
    g_scr[span:span + SUBLANES, :] = jnp.zeros((SUBLANES, d), F32)

    for r in range(1, SUBLANES):
        sh_scr[r - 1] = g_scr[r:r + span, :]

    def shifted(r, rows, cols):
        return g_scr[rows, cols] if r == 0 else sh_scr[r - 1, rows, cols]

    pad = kw // 2
    lw = 4 * LANES
    bias = b_ref[...]
    lg = lg_ref[...]
    lb = lb_ref[...]

    @pl.when(i == 0)
    def _():
        for j in range(kw):
            wb_scr[j] = jnp.broadcast_to(w_ref[j:j + 1, :], (SUBLANES, d))

    groups = ROW_CHUNK // SUBLANES

    def row_body(rc, carry):
        base = pl.multiple_of(rc * ROW_CHUNK, ROW_CHUNK)
        for ch in range(d // lw):
            cols = slice(ch * lw, (ch + 1) * lw)
            acc = [jnp.zeros((SUBLANES, lw), F32) for _ in range(groups)]
            for j in range(kw):
                off = j + halo - pad
                w = wb_scr[j, :, cols]
                for s in range(groups):
                    rows = pl.ds(pl.multiple_of(base + (off // SUBLANES + s) * SUBLANES, SUBLANES), SUBLANES)
                    acc[s] = acc[s] + w * shifted(off % SUBLANES, rows, cols)
            for s in range(groups):
                rows = pl.ds(pl.multiple_of(base + s * SUBLANES, SUBLANES), SUBLANES)
                c_scr[rows, cols] = acc[s] + bias[:, cols]
        y = c_scr[pl.ds(base, ROW_CHUNK), :]
        mu = jnp.mean(y, axis=-1, keepdims=True)
        var = jnp.mean(jnp.square(y - mu), axis=-1, keepdims=True)
        yn = (y - mu) * lax.rsqrt(var + NORM_EPS) * lg + lb
        o_ref[pl.ds(base, ROW_CHUNK), :] = _silu(yn).astype(BF16)
        return carry

    lax.fori_loop(0, tm // ROW_CHUNK, row_body, 0, unroll=2)


def _conv_module(proj, t, seq, d, dw_w, dw_b, ln_g, ln_b):
    kw = dw_w.shape[0]
    tm = min(256, seq)
    halo = CONV_HALO
    assert seq % tm == 0 and tm % halo == 0 and kw // 2 <= halo and d % (4 * LANES) == 0
    tiles_per_seq = seq // tm
    hpt = tm // halo
    n_halo = t // halo
    span = tm + 2 * halo

    def prev_map(col):
        return lambda i: (jnp.maximum(i * hpt - 1, 0), col)

    def next_map(col):
        return lambda i: (jnp.minimum((i + 1) * hpt, n_halo - 1), col)

    vec = lambda: pl.BlockSpec((1, d), lambda i: (0, 0))
    return pl.pallas_call(
        functools.partial(_conv_kernel, tiles_per_seq=tiles_per_seq, kw=kw),
        grid=(t // tm,),
        in_specs=[
            pl.BlockSpec((tm, d), lambda i: (i, 0)),
            pl.BlockSpec((tm, d), lambda i: (i, 1)),
            pl.BlockSpec((halo, d), prev_map(0)),
            pl.BlockSpec((halo, d), prev_map(1)),
            pl.BlockSpec((halo, d), next_map(0)),
            pl.BlockSpec((halo, d), next_map(1)),
            pl.BlockSpec((kw, d), lambda i: (0, 0)),
            vec(), vec(), vec(),
        ],
        out_specs=pl.BlockSpec((tm, d), lambda i: (i, 0)),
        out_shape=jax.ShapeDtypeStruct((t, d), BF16),
        scratch_shapes=[
            pltpu.VMEM((span + SUBLANES, d), F32),
            pltpu.VMEM((SUBLANES - 1, span, d), F32),
            pltpu.VMEM((tm, d), F32),
            pltpu.VMEM((kw, SUBLANES, d), F32),
        ],
        compiler_params=_params("arbitrary"),
        name="conformer_conv",
    )(proj, proj, proj, proj, proj, proj, dw_w, dw_b.reshape(1, d), ln_g.reshape(1, d), ln_b.reshape(1, d))


def kernel(x, c, ctx, c_ctx, norm_g, ada_w, ada_b, four_w_in, four_w_out, attn_w_in, attn_sink,
           attn_w_out, conv_w_in, conv_dw_w, conv_dw_b, conv_ln_g, conv_ln_b, conv_w_out, final_g):
    b, seq, d = x.shape
    n_ctx = ctx.shape[1]
    depth = norm_g.shape[0]
    kvw = N_KV_HEADS * HEAD_DIM
    assert d % kvw == 0 and seq % GRID_W == 0

    n_rows = -(-(b + 1) // SUBLANES) * SUBLANES
    cond = jnp.concatenate([c, c_ctx[None, :], jnp.zeros((n_rows - b - 1, d), F32)], axis=0)
    mod = _modulation(cond, ada_w, ada_b)

    cos, sin = _rope_tables(seq)
    qe = d
    ke = qe + kvw
    ve = ke + kvw
    attn_w_perm = jnp.concatenate([attn_w_in[:, :, ve:], attn_w_in[:, :, :ve]], axis=-1)

    x2 = x.reshape(b * seq, d)
    c2 = ctx.reshape(b * n_ctx, d)
    for i in range(depth):
        kind, j = i % N_MIXERS, i // N_MIXERS
        with_ctx = i < depth - 1
        last = i == depth - 1
        shift = mod[i, :, 0:d].reshape(n_rows, 1, d)
        scale = mod[i, :, d:2 * d].reshape(n_rows, 1, d)
        gate = mod[i, :, 2 * d:3 * d].reshape(n_rows, 1, d)
        if kind == 0:
            w_in = four_w_in[j].astype(BF16)
            w_out = four_w_out[j].astype(BF16)
            proj = _in_proj(x2, norm_g[i], shift, scale, w_in, seq, 0)
            a = _fourier_mixed(proj, b, seq, d, two_stage=True)
            z_col = 1
            if with_ctx:
                proj_c = _in_proj(c2, norm_g[i], shift, scale, w_in, b * n_ctx, b)
                a_c = _fourier_mixed(proj_c, b, n_ctx, d, two_stage=False)
        elif kind == 1:
            w_in = attn_w_perm[j].astype(BF16)
            w_out = attn_w_out[j].astype(BF16)
            z_col, q_col = 0, 1
            k_col = 2 * d // kvw
            v_col = k_col + 1
            proj = _in_proj(x2, norm_g[i], shift, scale, w_in, seq, 0)
            proj_c = _in_proj(c2, norm_g[i], shift, scale, w_in, b * n_ctx, b)
            k_rot = _k_rope(proj, b * seq, k_col, kvw, seq, cos, sin)
            a = _attention(attn_sink[j], proj, q_col, v_col, k_rot, proj_c, k_col, v_col,
                           b, seq, n_ctx, d, cos, sin)
            if with_ctx:
                a_c = _ctx_attention(attn_sink[j], proj_c, q_col, k_col, v_col, b, n_ctx, d)
        else:
            w_in = conv_w_in[j].astype(BF16)
            w_out = conv_w_out[j].astype(BF16)
            z_col = 2
            conv_args = (conv_dw_w[j], conv_dw_b[j], conv_ln_g[j], conv_ln_b[j])
            proj = _in_proj(x2, norm_g[i], shift, scale, w_in, seq, 0)
            a = _conv_module(proj, b * seq, seq, d, *conv_args)
            if with_ctx:
                proj_c = _in_proj(c2, norm_g[i], shift, scale, w_in, b * n_ctx, b)
                a_c = _conv_module(proj_c, b * n_ctx, n_ctx, d, *conv_args)
        x2 = _out_proj(a, proj, z_col, x2, gate, w_out, final_g, seq, 0, final_norm=last)
        if with_ctx:
            c2 = _out_proj(a_c, proj_c, z_col, c2, gate, w_out, final_g, b * n_ctx, b, final_norm=False)
    return x2.reshape(b, seq, d)
```

```python
import functools
import math

import jax
import jax.numpy as jnp
from jax import lax
from jax.experimental import pallas as pl
from jax.experimental.pallas import tpu as pltpu

HEAD_DIM = 64
N_KV_HEADS = 4
FOURIER_GROUPS = 8
WINDOW = 128
GRID_W = 64
N_MIXERS = 3
NORM_EPS = 1e-6
ROPE_BASE = 10000.0
NEG_INF = -1e30
ATTN_SCALE = HEAD_DIM ** -0.5
LOG2_E = math.log2(math.e)

LANES = 128
SUBLANES = 8
MXU_COLS = 256
VMEM_LIMIT_BYTES = 56 * 1024 * 1024

DFT_N1 = 128
IN_PROJ_COL_TILES = 4
INTERLEAVE = 4
SCORE_LOOKAHEAD = 1
ROW_CHUNK = 16
ROW_UNROLL = 4
CONV_HALO = 16

F32 = jnp.float32
BF16 = jnp.bfloat16


def _params(*sem):
    return pltpu.CompilerParams(dimension_semantics=sem, vmem_limit_bytes=VMEM_LIMIT_BYTES)


def _silu(v):
    return v * jax.nn.sigmoid(v)


def _col_pieces(n):
    units = -(-n // MXU_COLS)
    p = min(INTERLEAVE, units)
    bounds = [min(n, (k * units // p) * MXU_COLS) for k in range(p + 1)]
    return list(zip(bounds[:-1], bounds[1:]))


def _pick_tile(n, cap, quantum):
    if n <= cap:
        return n
    t = cap - cap % quantum
    while n % t:
        t -= quantum
    return t


def _mod_kernel(cond_ref, w_ref, b_ref, o_ref):
    s = _silu(cond_ref[...]).astype(BF16)
    o_ref[0] = jnp.dot(s, w_ref[0].astype(BF16), preferred_element_type=F32) + b_ref[0]


def _modulation(cond, ada_w, ada_b):
    depth, d, d3 = ada_w.shape
    r = cond.shape[0]
    tn = _pick_tile(d3, 1024, 2 * LANES)
    return pl.pallas_call(
        _mod_kernel,
        grid=(depth, d3 // tn),
        in_specs=[
            pl.BlockSpec((r, d), lambda l, j: (0, 0)),
            pl.BlockSpec((1, d, tn), lambda l, j: (l, 0, j)),
            pl.BlockSpec((1, 1, tn), lambda l, j: (l, 0, j)),
        ],
        out_specs=pl.BlockSpec((1, r, tn), lambda l, j: (l, 0, j)),
        out_shape=jax.ShapeDtypeStruct((depth, r, d3), F32),
        compiler_params=_params("arbitrary", "arbitrary"),
        name="ada_modulation",
    )(cond, ada_w, ada_b.reshape(depth, 1, d3))


def _in_proj_kernel(x_ref, g_ref, shift_ref, scale_ref, w_ref, o_ref, h0_scr, h1_scr):
    i = pl.program_id(0)
    j = pl.program_id(1)
    r = x_ref.shape[0]

    @pl.when((i == 0) & (j == 0))
    def _():
        h1_scr[...] = jnp.zeros(h1_scr.shape, BF16)

    def step(h_write, h_read):
        g = g_ref[...]
        sh = shift_ref[0]
        sc = 1.0 + scale_ref[0]
        tn = w_ref.shape[1]
        n_chunks = r // ROW_CHUNK
        pieces = _col_pieces(tn)
        for piece, (lo, hi) in enumerate(pieces):
            cols = slice(lo, hi)
            o_ref[:, cols] = jnp.dot(h_read[...], w_ref[:, cols], preferred_element_type=F32).astype(BF16)
            for c in range(piece * n_chunks // len(pieces), (piece + 1) * n_chunks // len(pieces)):
                xf = x_ref[c * ROW_CHUNK:(c + 1) * ROW_CHUNK, :]
                ms = jnp.mean(xf * xf, axis=-1, keepdims=True)
                y = xf * lax.rsqrt(ms + NORM_EPS)
                rows = pl.ds(pl.multiple_of(j * r + c * ROW_CHUNK, ROW_CHUNK), ROW_CHUNK)
                h_write[rows, :] = ((y * g) * sc + sh).astype(BF16)

    @pl.when(i % 2 == 0)
    def _():
        step(h0_scr, h1_scr)

    @pl.when(i % 2 == 1)
    def _():
        step(h1_scr, h0_scr)


def _in_proj(x2, norm_g, shift, scale, w_bf16, rows_per_mod, mod_row0):
    t, d = x2.shape
    n_out = w_bf16.shape[1]
    tm = min(1024, rows_per_mod, t)
    nn = IN_PROJ_COL_TILES
    if (n_out // nn) % MXU_COLS and (n_out // (nn // 2)) % MXU_COLS == 0 and (tm // 2) % (nn // 2 * ROW_CHUNK) == 0:
        tm, nn = tm // 2, nn // 2
    tn = n_out // nn
    r = tm // nn
    assert t % tm == 0 and rows_per_mod % tm == 0 and n_out % nn == 0 and tn % LANES == 0 and r % ROW_CHUNK == 0
    nm = t // tm
    tiles_per_mod = rows_per_mod // tm
    norm_tile = lambda i: jnp.minimum(i, nm - 1)
    mod_map = lambda i, j: (mod_row0 + norm_tile(i) // tiles_per_mod, 0, 0)
    return pl.pallas_call(
        _in_proj_kernel,
        grid=(nm + 1, nn),
        in_specs=[
            pl.BlockSpec((r, d), lambda i, j: (norm_tile(i) * nn + j, 0)),
            pl.BlockSpec((1, d), lambda i, j: (0, 0)),
            pl.BlockSpec((1, 1, d), mod_map),
            pl.BlockSpec((1, 1, d), mod_map),
            pl.BlockSpec((d, tn), lambda i, j: (0, j)),
        ],
        out_specs=pl.BlockSpec((tm, tn), lambda i, j: (jnp.where(i == 0, nm, i - 1), j)),
        out_shape=jax.ShapeDtypeStruct((t + tm, n_out), BF16),
        scratch_shapes=[pltpu.VMEM((tm, d), BF16), pltpu.VMEM((tm, d), BF16)],
        compiler_params=_params("arbitrary", "arbitrary"),
        name="norm_in_proj",
    )(x2, norm_g.reshape(1, d), shift, scale, w_bf16)


def _out_proj_kernel(a_ref, z_ref, x_ref, gate_ref, w_ref, fg_ref, o_ref, y0_scr, y1_scr, *, final_norm, a_run):
    i = pl.program_id(0)
    tm = x_ref.shape[0]

    @pl.when(i == 0)
    def _():
        y1_scr[...] = jnp.zeros(y1_scr.shape, BF16)

    def step(y_write, y_read):
        d = w_ref.shape[1]
        n_chunks = tm // ROW_CHUNK
        gate = gate_ref[0]
        pieces = _col_pieces(d)
        for piece, (lo, hi) in enumerate(pieces):
            cols = slice(lo, hi)
            o = jnp.dot(y_read[...], w_ref[:, cols], preferred_element_type=F32)
            o_ref[:, cols] = x_ref[:, cols] + gate[:, cols] * o
            for c in range(piece * n_chunks // len(pieces), (piece + 1) * n_chunks // len(pieces)):
                rows = slice(c * ROW_CHUNK, (c + 1) * ROW_CHUNK)
                if len(a_ref.shape) == 3:
                    off = (c * ROW_CHUNK) % a_run
                    a = a_ref[(c * ROW_CHUNK) // a_run, off:off + ROW_CHUNK, :]
                else:
                    a = a_ref[rows, :]
                y_write[rows, :] = (a.astype(F32) * _silu(z_ref[rows, :].astype(F32))).astype(BF16)
        if final_norm:
            fg = fg_ref[...]
            for c in range(n_chunks):
                rows = slice(c * ROW_CHUNK, (c + 1) * ROW_CHUNK)
                xn = o_ref[rows, :]
                ms = jnp.mean(xn * xn, axis=-1, keepdims=True)
                o_ref[rows, :] = (xn * lax.rsqrt(ms + NORM_EPS)) * fg

    @pl.when(i % 2 == 0)
    def _():
        step(y0_scr, y1_scr)

    @pl.when(i % 2 == 1)
    def _():
        step(y1_scr, y0_scr)


def _out_proj(a, z_arr, z_col, x2, gate, w_bf16, final_g, rows_per_mod, mod_row0, final_norm):
    t, d = x2.shape
    tm = min(512, rows_per_mod, t)
    assert t % tm == 0 and rows_per_mod % tm == 0
    nm = t // tm
    tiles_per_mod = rows_per_mod // tm
    gate_tile = lambda i: jnp.minimum(i, nm - 1)
    mm_tile = lambda i: jnp.maximum(i - 1, 0)
    a_run = None
    a_spec = pl.BlockSpec((tm, d), lambda i: (gate_tile(i), 0))
    if a.ndim == 3:
        a_run = t // a.shape[0]
        assert tm % a_run == 0 and a_run % ROW_CHUNK == 0
        a_spec = pl.BlockSpec((tm // a_run, a.shape[1], d), lambda i: (gate_tile(i), 0, 0))
    return pl.pallas_call(
        functools.partial(_out_proj_kernel, final_norm=final_norm, a_run=a_run),
        grid=(nm + 1,),
        in_specs=[
            a_spec,
            pl.BlockSpec((tm, d), lambda i: (gate_tile(i), z_col)),
            pl.BlockSpec((tm, d), lambda i: (mm_tile(i), 0)),
            pl.BlockSpec((1, 1, d), lambda i: (mod_row0 + mm_tile(i) // tiles_per_mod, 0, 0)),
            pl.BlockSpec((d, d), lambda i: (0, 0), pipeline_mode=pl.Buffered(1)),
            pl.BlockSpec((1, d), lambda i: (0, 0)),
        ],
        out_specs=pl.BlockSpec((tm, d), lambda i: (mm_tile(i), 0)),
        out_shape=jax.ShapeDtypeStruct((t, d), F32),
        scratch_shapes=[pltpu.VMEM((tm, d), BF16), pltpu.VMEM((tm, d), BF16)],
        compiler_params=_params("arbitrary"),
        name="gated_out_proj",
    )(a, z_arr, x2, gate, w_bf16, final_g.reshape(1, d))


def _cos_sin(n_rows, n_cols, period):
    r = jnp.arange(n_rows, dtype=jnp.int32)[:, None]
    c = jnp.arange(n_cols, dtype=jnp.int32)[None, :]
    ang = ((r * c) % period).astype(F32) * (2.0 * math.pi / period)
    return jnp.cos(ang), jnp.sin(ang)


def _pack_complex(re, im):
    hi = lax.bitcast_convert_type(re.astype(BF16).astype(F32), jnp.uint32)
    lo = lax.bitcast_convert_type(im.astype(BF16).astype(F32), jnp.uint32)
    return hi | (lo >> 16)


def _unpack_complex(w):
    re = lax.bitcast_convert_type(w & jnp.uint32(0xFFFF0000), F32)
    im = lax.bitcast_convert_type(w << 16, F32)
    return re.astype(BF16), im.astype(BF16)


def _chan_dft_kernel(u_ref, m_ref, z_ref, *, groups, run):
    gs = m_ref.shape[0]
    tm = u_ref.shape[0]
    for g in range(groups):
        cols = slice(g * gs, (g + 1) * gs)
        r = jnp.dot(u_ref[:, cols], m_ref[...], preferred_element_type=F32)
        packed = _pack_complex(r[:, :gs], r[:, gs:])
        if run is None:
            z_ref[:, cols] = packed
        else:
            for k in range(tm // run):
                z_ref[k, 0:run, cols] = packed[k * run:(k + 1) * run]
    if run is not None:
        pad = z_ref.shape[1] - run
        z_ref[:, run:, :] = jnp.zeros((tm // run, pad, z_ref.shape[2]), jnp.uint32)


def _chan_dft(proj, t, d, run=None, pitch=None):
    gs = d // FOURIER_GROUPS
    c, s = _cos_sin(gs, gs, gs)
    m = jnp.concatenate([c, -s], axis=1).astype(BF16)
    tm = min(1024, t)
    assert t % tm == 0
    if run is None:
        out_spec = pl.BlockSpec((tm, d), lambda i: (i, 0))
        out_shape = jax.ShapeDtypeStruct((t, d), jnp.uint32)
    else:
        assert tm % run == 0
        out_spec = pl.BlockSpec((tm // run, pitch, d), lambda i: (i, 0, 0))
        out_shape = jax.ShapeDtypeStruct((t // run, pitch, d), jnp.uint32)
    return pl.pallas_call(
        functools.partial(_chan_dft_kernel, groups=FOURIER_GROUPS, run=run),
        grid=(t // tm,),
        in_specs=[
            pl.BlockSpec((tm, d), lambda i: (i, 0)),
            pl.BlockSpec((gs, 2 * gs), lambda i: (0, 0)),
        ],
        out_specs=out_spec,
        out_shape=out_shape,
        compiler_params=_params("arbitrary"),
        name="fourier_chan_dft",
    )(proj, m)


def _stack_pair(parts):
    return jnp.concatenate([jnp.concatenate([parts[0][0], parts[1][0]], axis=1),
                            jnp.concatenate([parts[0][1], parts[1][1]], axis=1)], axis=0)


def _seq_dft_kernel(z_ref, m1_ref, m2_ref, twc_ref, tws_ref, o_ref, y_scr, *, n1, n2, pitch, opitch, inv_norm):
    twc = twc_ref[...]
    tws = tws_ref[...]
    for q in range(0, n2, 2):
        zin = _stack_pair([_unpack_complex(z_ref[0, pl.ds(qq, n1, stride=pitch), :]) for qq in (q, q + 1)])
        y = jnp.dot(m1_ref[...], zin, preferred_element_type=F32)
        for t, qq in enumerate((q, q + 1)):
            yr = y[:n1, t * LANES:(t + 1) * LANES]
            yi = y[n1:, t * LANES:(t + 1) * LANES]
            c = twc[:, qq:qq + 1]
            s = tws[:, qq:qq + 1]
            y_scr[pl.ds(qq, n1, stride=pitch), :] = _pack_complex(yr * c + yi * s, yi * c - yr * s)
    for k in range(0, n1, 2):
        yin = _stack_pair([_unpack_complex(y_scr[kk * pitch:kk * pitch + n2, :]) for kk in (k, k + 1)])
        r = jnp.dot(m2_ref[...], yin, preferred_element_type=F32) * inv_norm
        for t, kk in enumerate((k, k + 1)):
            o_ref[0, pl.ds(kk, n2, stride=opitch), :] = r[:, t * LANES:(t + 1) * LANES]
    for j in range(n1, opitch):
        o_ref[0, pl.ds(j, n2, stride=opitch), :] = jnp.zeros((n2, LANES), F32)


def _seq_dft(zp, b, seq, d, pitch, opitch, inv_norm):
    n1 = DFT_N1
    n2 = seq // n1
    assert seq % n1 == 0 and n1 % 2 == 0 and n2 % 2 == 0 and d % LANES == 0
    c1, s1 = _cos_sin(n1, n1, n1)
    m1 = jnp.concatenate([jnp.concatenate([c1, s1], axis=1),
                          jnp.concatenate([-s1, c1], axis=1)], axis=0).astype(BF16)
    c2, s2 = _cos_sin(n2, n2, n2)
    m2 = jnp.concatenate([c2, s2], axis=1).astype(BF16)
    twc, tws = _cos_sin(n1, n2, seq)
    const = lambda shape: pl.BlockSpec(shape, lambda bi, k: (0, 0))
    out = pl.pallas_call(
        functools.partial(_seq_dft_kernel, n1=n1, n2=n2, pitch=pitch, opitch=opitch, inv_norm=inv_norm),
        grid=(b, d // LANES),
        in_specs=[
            pl.BlockSpec((1, n1 * pitch, LANES), lambda bi, k: (bi, 0, k)),
            const((2 * n1, 2 * n1)), const((n2, 2 * n2)), const((n1, n2)), const((n1, n2)),
        ],
        out_specs=pl.BlockSpec((1, n2 * opitch, LANES), lambda bi, k: (bi, 0, k)),
        out_shape=jax.ShapeDtypeStruct((b, n2 * opitch, d), F32),
        scratch_shapes=[pltpu.VMEM((n1 * pitch, LANES), jnp.uint32)],
        compiler_params=_params("arbitrary", "arbitrary"),
        name="fourier_seq_dft",
    )(zp.reshape(b, n1 * pitch, d), m1, m2, twc, tws)
    return out.reshape(b * n2, opitch, d)


def _seq_dft_direct_kernel(z_ref, m_ref, o_ref, *, inv_norm):
    zr, zi = _unpack_complex(z_ref[0])
    o_ref[0] = jnp.dot(m_ref[...], jnp.concatenate([zr, zi], axis=0), preferred_element_type=F32) * inv_norm


def _seq_dft_direct(z, b, seq, d, inv_norm):
    dt = min(1024, d)
    c, s = _cos_sin(seq, seq, seq)
    m = jnp.concatenate([c, s], axis=1).astype(BF16)
    out = pl.pallas_call(
        functools.partial(_seq_dft_direct_kernel, inv_norm=inv_norm),
        grid=(b, d // dt),
        in_specs=[
            pl.BlockSpec((1, seq, dt), lambda bi, k: (bi, 0, k)),
            pl.BlockSpec((seq, 2 * seq), lambda bi, k: (0, 0)),
        ],
        out_specs=pl.BlockSpec((1, seq, dt), lambda bi, k: (bi, 0, k)),
        out_shape=jax.ShapeDtypeStruct((b, seq, d), F32),
        compiler_params=_params("arbitrary", "arbitrary"),
        name="fourier_seq_dft_direct",
    )(z.reshape(b, seq, d), m)
    return out.reshape(b * seq, d)


def _fourier_mixed(proj, b, seq, d, two_stage):
    inv_norm = 1.0 / math.sqrt(seq * (d // FOURIER_GROUPS))
    if two_stage:
        n2 = seq // DFT_N1
        pitch = n2 + SUBLANES
        z = _chan_dft(proj, b * seq, d, run=n2, pitch=pitch)
        return _seq_dft(z, b, seq, d, pitch, DFT_N1 + SUBLANES, inv_norm)
    return _seq_dft_direct(_chan_dft(proj, b * seq, d), b, seq, d, inv_norm)


def _rope_tables(n_tokens):
    rows = n_tokens // GRID_W
    row = jnp.repeat(jnp.arange(rows), GRID_W).astype(F32)
    col = jnp.tile(jnp.arange(GRID_W), rows).astype(F32)
    quarter = HEAD_DIM // 4
    inv_freq = ROPE_BASE ** (-jnp.arange(quarter, dtype=F32) / quarter)
    ang_r = row[:, None] * inv_freq[None, :]
    ang_c = col[:, None] * inv_freq[None, :]
    ang = jnp.concatenate([ang_r, ang_r, ang_c, ang_c], axis=-1)
    sign = jnp.where((jnp.arange(HEAD_DIM) % (HEAD_DIM // 2)) < quarter, -1.0, 1.0).astype(F32)
    reps = LANES // HEAD_DIM
    return jnp.tile(jnp.cos(ang), (1, reps)), jnp.tile(jnp.sin(ang) * sign, (1, reps))


def _rope128(xf, cos, sin_signed):
    quarter = HEAD_DIM // 4
    lane = lax.broadcasted_iota(jnp.int32, xf.shape, 1)
    from_hi = pltpu.roll(xf, LANES - quarter, axis=1)
    from_lo = pltpu.roll(xf, quarter, axis=1)
    partner = jnp.where((lane % (HEAD_DIM // 2)) < quarter, from_hi, from_lo)
    return xf * cos + partner * sin_signed


def _k_rope_kernel(k_ref, cos_ref, sin_ref, o_ref):
    cos = cos_ref[...]
    sin = sin_ref[...]
    for c in range(k_ref.shape[1] // LANES):
        cols = slice(c * LANES, (c + 1) * LANES)
        o_ref[:, cols] = _rope128(k_ref[:, cols].astype(F32), cos, sin).astype(BF16)


def _k_rope(proj, t, k_col, kvw, seq, cos, sin):
    tm = min(1024, seq)
    assert seq % tm == 0 and kvw % LANES == 0
    tiles_per_seq = seq // tm
    return pl.pallas_call(
        _k_rope_kernel,
        grid=(t // tm,),
        in_specs=[
            pl.BlockSpec((tm, kvw), lambda i: (i, k_col)),
            pl.BlockSpec((tm, LANES), lambda i: (i % tiles_per_seq, 0)),
            pl.BlockSpec((tm, LANES), lambda i: (i % tiles_per_seq, 0)),
        ],
        out_specs=pl.BlockSpec((tm, kvw), lambda i: (i, 0)),
        out_shape=jax.ShapeDtypeStruct((t, kvw), BF16),
        compiler_params=_params("arbitrary"),
        name="attn_k_rope",
    )(proj, cos, sin)


def _head_halves(cat_f32, lo):
    lane = lax.broadcasted_iota(jnp.int32, cat_f32.shape, 1)
    same = jnp.where((lane >= lo) & (lane < lo + HEAD_DIM), cat_f32, 0.0)
    swap = pltpu.roll(same, HEAD_DIM, axis=1)
    low, high = (same, swap) if lo == 0 else (swap, same)
    return low.astype(BF16), high.astype(BF16)


def _attn_kernel(*refs, band, bq, seq, n_ctx, groups):
    if band:
        (sink_ref, q_ref, cos_ref, sin_ref, kp_ref, kc_ref, kn_ref,
         vp_ref, vc_ref, vn_ref, kx_ref, vx_ref, o_ref, cap_scr) = refs
    else:
        sink_ref, q_ref, kx_ref, vx_ref, o_ref = refs
    n_band = bq + 2 * WINDOW if band else 0
    nk = n_band + n_ctx
    if band:
        start = pl.program_id(1) * bq
        qpos = start + lax.broadcasted_iota(jnp.int32, (bq, nk), 0)
        col = lax.broadcasted_iota(jnp.int32, (bq, nk), 1)
        kpos = start - WINDOW + col
        valid = ((jnp.abs(qpos - kpos) <= WINDOW) & (kpos >= 0) & (kpos < seq)) | (col >= n_band)
        cap_scr[...] = jnp.where(valid, jnp.inf, NEG_INF)
        cos = cos_ref[...]
        sin = sin_ref[...]
    lane_q = lax.broadcasted_iota(jnp.int32, (bq, LANES), 1)
    nt = (((1,), (1,)), ((), ()))

    def probs(s, sink2):
        if band:
            s = jnp.minimum(s, cap_scr[...])
        m = jnp.maximum(jnp.max(s, axis=-1, keepdims=True), sink2)
        p = jnp.exp2(s - m)
        denom = jnp.sum(p, axis=-1, keepdims=True) + jnp.exp2(sink2 - m)
        return p.astype(BF16), 1.0 / denom

    def kv_halves(h):
        pair_cols = slice((h // 2) * LANES, (h // 2 + 1) * LANES)
        k_refs = (kp_ref, kc_ref, kn_ref, kx_ref) if band else (kx_ref,)
        v_refs = (vp_ref, vc_ref, vn_ref, vx_ref) if band else (vx_ref,)
        kcat = jnp.concatenate([r[:, pair_cols].astype(F32) for r in k_refs], axis=0)
        vcat = jnp.concatenate([r[:, pair_cols].astype(F32) for r in v_refs], axis=0)
        lo = (h % 2) * HEAD_DIM
        return _head_halves(kcat, lo) + _head_halves(vcat, lo)

    def scores(h, p, kv):
        off = (h * groups + 2 * p) * HEAD_DIM
        qf = q_ref[:, off:off + LANES].astype(F32)
        if band:
            qf = _rope128(qf, cos, sin)
        qb = (qf * (ATTN_SCALE * LOG2_E)).astype(BF16)
        return (lax.dot_general(qb, kv[0], nt, preferred_element_type=F32),
                lax.dot_general(qb, kv[1], nt, preferred_element_type=F32))

    def finish(h, p, kv, s0, s1):
        off = (h * groups + 2 * p) * HEAD_DIM
        p0, inv0 = probs(s0, sink_ref[h * groups + 2 * p] * LOG2_E)
        p1, inv1 = probs(s1, sink_ref[h * groups + 2 * p + 1] * LOG2_E)
        o_pair = (jnp.dot(p0, kv[2], preferred_element_type=F32)
                  + jnp.dot(p1, kv[3], preferred_element_type=F32))
        o_ref[:, off:off + LANES] = (o_pair * jnp.where(lane_q < HEAD_DIM, inv0, inv1)).astype(BF16)

    pairs = [(h, p) for h in range(N_KV_HEADS) for p in range(groups // 2)]
    kv = {}
    ready = []

    def issue(idx):
        hn, pn = pairs[idx]
        if hn not in kv:
            kv[hn] = kv_halves(hn)
        ready.append(scores(hn, pn, kv[hn]))

    for idx in range(min(SCORE_LOOKAHEAD, len(pairs))):
        issue(idx)
    for idx, (h, p) in enumerate(pairs):
        if idx + SCORE_LOOKAHEAD < len(pairs):
            issue(idx + SCORE_LOOKAHEAD)
        finish(h, p, kv[h], *ready[idx])


def _attention(sink, proj, q_col, v_col, k_rot, proj_c, kc_col, vc_col, b, seq, n_ctx, d, cos, sin):
    kvw = N_KV_HEADS * HEAD_DIM
    groups = d // kvw
    bq = min(256, seq)
    assert seq % bq == 0 and bq % WINDOW == 0 and n_ctx % WINDOW == 0 and groups % 2 == 0 and kvw % LANES == 0
    nq = seq // bq
    wpb = bq // WINDOW
    n_halo = b * seq // WINDOW

    def prev_map(col):
        return lambda bi, i: (jnp.maximum((bi * nq + i) * wpb - 1, 0), col)

    def next_map(col):
        return lambda bi, i: (jnp.minimum((bi * nq + i + 1) * wpb, n_halo - 1), col)

    def cur_map(col):
        return lambda bi, i: (bi * nq + i, col)

    return pl.pallas_call(
        functools.partial(_attn_kernel, band=True, bq=bq, seq=seq, n_ctx=n_ctx, groups=groups),
        grid=(b, nq),
        in_specs=[
            pl.BlockSpec(memory_space=pltpu.SMEM),
            pl.BlockSpec((bq, d), cur_map(q_col)),
            pl.BlockSpec((bq, LANES), lambda bi, i: (i, 0)),
            pl.BlockSpec((bq, LANES), lambda bi, i: (i, 0)),
            pl.BlockSpec((WINDOW, kvw), prev_map(0)),
            pl.BlockSpec((bq, kvw), cur_map(0)),
            pl.BlockSpec((WINDOW, kvw), next_map(0)),
            pl.BlockSpec((WINDOW, kvw), prev_map(v_col)),
            pl.BlockSpec((bq, kvw), cur_map(v_col)),
            pl.BlockSpec((WINDOW, kvw), next_map(v_col)),
            pl.BlockSpec((n_ctx, kvw), lambda bi, i: (bi, kc_col)),
            pl.BlockSpec((n_ctx, kvw), lambda bi, i: (bi, vc_col)),
        ],
        out_specs=pl.BlockSpec((bq, d), lambda bi, i: (bi * nq + i, 0)),
        out_shape=jax.ShapeDtypeStruct((b * seq, d), BF16),
        scratch_shapes=[pltpu.VMEM((bq, bq + 2 * WINDOW + n_ctx), F32)],
        compiler_params=_params("arbitrary", "arbitrary"),
        name="window_attention",
    )(sink, proj, cos, sin, k_rot, k_rot, k_rot, proj, proj, proj, proj_c, proj_c)


def _ctx_attention(sink, proj_c, q_col, k_col, v_col, b, n_ctx, d):
    kvw = N_KV_HEADS * HEAD_DIM
    groups = d // kvw
    return pl.pallas_call(
        functools.partial(_attn_kernel, band=False, bq=n_ctx, seq=n_ctx, n_ctx=n_ctx, groups=groups),
        grid=(b,),
        in_specs=[
            pl.BlockSpec(memory_space=pltpu.SMEM),
            pl.BlockSpec((n_ctx, d), lambda bi: (bi, q_col)),
            pl.BlockSpec((n_ctx, kvw), lambda bi: (bi, k_col)),
            pl.BlockSpec((n_ctx, kvw), lambda bi: (bi, v_col)),
        ],
        out_specs=pl.BlockSpec((n_ctx, d), lambda bi: (bi, 0)),
        out_shape=jax.ShapeDtypeStruct((b * n_ctx, d), BF16),
        compiler_params=_params("arbitrary"),
        name="context_attention",
    )(sink, proj_c, proj_c, proj_c)


def _conv_kernel(a_ref, ag_ref, ap_ref, agp_ref, an_ref, agn_ref, w_ref, b_ref, lg_ref, lb_ref,
                 o_ref, g_scr, sh_scr, c_scr, wb_scr, *, tiles_per_seq, kw):
    tm, d = a_ref.shape
    halo = CONV_HALO
    span = tm + 2 * halo
    i = pl.program_id(0)
    first = (i % tiles_per_seq) == 0
    last = (i % tiles_per_seq) == tiles_per_seq - 1

    def glu(a, ag):
        return a.astype(F32) * jax.nn.sigmoid(ag.astype(F32))

    g_scr[0:halo, :] = jnp.where(first, 0.0, glu(ap_ref[...], agp_ref[...]))

    def glu_body(r, carry):
        rows = pl.ds(pl.multiple_of(r * ROW_CHUNK, ROW_CHUNK), ROW_CHUNK)
        g_scr[pl.ds(pl.multiple_of(halo + r * ROW_CHUNK, ROW_CHUNK), ROW_CHUNK), :] = glu(a_ref[rows, :], ag_ref[rows, :])
        return carry

    lax.fori_loop(0, tm // ROW_CHUNK, glu_body, 0, unroll=ROW_UNROLL)
    g_scr[halo + tm:span, :] = jnp.where(last, 0.0, glu(an_ref[...], agn_ref[...]))
    g_scr[span:span + SUBLANES, :] = jnp.zeros((SUBLANES, d), F32)

    for r in range(1, SUBLANES):
        sh_scr[r - 1] = g_scr[r:r + span, :]

    def shifted(r, rows, cols):
        return g_scr[rows, cols] if r == 0 else sh_scr[r - 1, rows, cols]

    pad = kw // 2
    lw = 4 * LANES
    bias = b_ref[...]
    lg = lg_ref[...]
    lb = lb_ref[...]

    @pl.when(i == 0)
    def _():
        for j in range(kw):
            wb_scr[j] = jnp.broadcast_to(w_ref[j:j + 1, :], (SUBLANES, d))

    groups = ROW_CHUNK // SUBLANES

    def row_body(rc, carry):
        base = pl.multiple_of(rc * ROW_CHUNK, ROW_CHUNK)
        for ch in range(d // lw):
            cols = slice(ch * lw, (ch + 1) * lw)
            acc = [jnp.zeros((SUBLANES, lw), F32) for _ in range(groups)]
            for j in range(kw):
                off = j + halo - pad
                w = wb_scr[j, :, cols]
                for s in range(groups):
                    rows = pl.ds(pl.multiple_of(base + (off // SUBLANES + s) * SUBLANES, SUBLANES), SUBLANES)
                    acc[s] = acc[s] + w * shifted(off % SUBLANES, rows, cols)
            for s in range(groups):
                rows = pl.ds(pl.multiple_of(base + s * SUBLANES, SUBLANES), SUBLANES)
                c_scr[rows, cols] = acc[s] + bias[:, cols]
        y = c_scr[pl.ds(base, ROW_CHUNK), :]
        mu = jnp.mean(y, axis=-1, keepdims=True)
        var = jnp.mean(jnp.square(y - mu), axis=-1, keepdims=True)
        yn = (y - mu) * lax.rsqrt(var + NORM_EPS) * lg + lb
        o_ref[pl.ds(base, ROW_CHUNK), :] = _silu(yn).astype(BF16)
        return carry

    lax.fori_loop(0, tm // ROW_CHUNK, row_body, 0, unroll=2)


def _conv_module(proj, t, seq, d, dw_w, dw_b, ln_g, ln_b):
    kw = dw_w.shape[0]
    tm = min(256, seq)
    halo = CONV_HALO
    assert seq % tm == 0 and tm % halo == 0 and kw // 2 <= halo and d % (4 * LANES) == 0
    tiles_per_seq = seq // tm
    hpt = tm // halo
    n_halo = t // halo
    span = tm + 2 * halo

    def prev_map(col):
        return lambda i: (jnp.maximum(i * hpt - 1, 0), col)

    def next_map(col):
        return lambda i: (jnp.minimum((i + 1) * hpt, n_halo - 1), col)

    vec = lambda: pl.BlockSpec((1, d), lambda i: (0, 0))
    return pl.pallas_call(
        functools.partial(_conv_kernel, tiles_per_seq=tiles_per_seq, kw=kw),
        grid=(t // tm,),
        in_specs=[
            pl.BlockSpec((tm, d), lambda i: (i, 0)),
            pl.BlockSpec((tm, d), lambda i: (i, 1)),
            pl.BlockSpec((halo, d), prev_map(0)),
            pl.BlockSpec((halo, d), prev_map(1)),
            pl.BlockSpec((halo, d), next_map(0)),
            pl.BlockSpec((halo, d), next_map(1)),
            pl.BlockSpec((kw, d), lambda i: (0, 0)),
            vec(), vec(), vec(),
        ],
        out_specs=pl.BlockSpec((tm, d), lambda i: (i, 0)),
        out_shape=jax.ShapeDtypeStruct((t, d), BF16),
        scratch_shapes=[
            pltpu.VMEM((span + SUBLANES, d), F32),
            pltpu.VMEM((SUBLANES - 1, span, d), F32),
            pltpu.VMEM((tm, d), F32),
            pltpu.VMEM((kw, SUBLANES, d), F32),
        ],
        compiler_params=_params("arbitrary"),
        name="conformer_conv",
    )(proj, proj, proj, proj, proj, proj, dw_w, dw_b.reshape(1, d), ln_g.reshape(1, d), ln_b.reshape(1, d))


def kernel(x, c, ctx, c_ctx, norm_g, ada_w, ada_b, four_w_in, four_w_out, attn_w_in, attn_sink,
           attn_w_out, conv_w_in, conv_dw_w, conv_dw_b, conv_ln_g, conv_ln_b, conv_w_out, final_g):
    b, seq, d = x.shape
    n_ctx = ctx.shape[1]
    depth = norm_g.shape[0]
    kvw = N_KV_HEADS * HEAD_DIM
    assert d % kvw == 0 and seq % GRID_W == 0

    n_rows = -(-(b + 1) // SUBLANES) * SUBLANES
    cond = jnp.concatenate([c, c_ctx[None, :], jnp.zeros((n_rows - b - 1, d), F32)], axis=0)
    mod = _modulation(cond, ada_w, ada_b)

    cos, sin = _rope_tables(seq)
    qe = d
    ke = qe + kvw
    ve = ke + kvw
    attn_w_perm = jnp.concatenate([attn_w_in[:, :, ve:], attn_w_in[:, :, :ve]], axis=-1)

    x2 = x.reshape(b * seq, d)
    c2 = ctx.reshape(b * n_ctx, d)
    for i in range(depth):
        kind, j = i % N_MIXERS, i // N_MIXERS
        with_ctx = i < depth - 1
        last = i == depth - 1
        shift = mod[i, :, 0:d].reshape(n_rows, 1, d)
        scale = mod[i, :, d:2 * d].reshape(n_rows, 1, d)
        gate = mod[i, :, 2 * d:3 * d].reshape(n_rows, 1, d)
        if kind == 0:
            w_in = four_w_in[j].astype(BF16)
            w_out = four_w_out[j].astype(BF16)
            proj = _in_proj(x2, norm_g[i], shift, scale, w_in, seq, 0)
            a = _fourier_mixed(proj, b, seq, d, two_stage=True)
            z_col = 1
            if with_ctx:
                proj_c = _in_proj(c2, norm_g[i], shift, scale, w_in, b * n_ctx, b)
                a_c = _fourier_mixed(proj_c, b, n_ctx, d, two_stage=False)
        elif kind == 1:
            w_in = attn_w_perm[j].astype(BF16)
            w_out = attn_w_out[j].astype(BF16)
            z_col, q_col = 0, 1
            k_col = 2 * d // kvw
            v_col = k_col + 1
            proj = _in_proj(x2, norm_g[i], shift, scale, w_in, seq, 0)
            proj_c = _in_proj(c2, norm_g[i], shift, scale, w_in, b * n_ctx, b)
            k_rot = _k_rope(proj, b * seq, k_col, kvw, seq, cos, sin)
            a = _attention(attn_sink[j], proj, q_col, v_col, k_rot, proj_c, k_col, v_col,
                           b, seq, n_ctx, d, cos, sin)
            if with_ctx:
                a_c = _ctx_attention(attn_sink[j], proj_c, q_col, k_col, v_col, b, n_ctx, d)
        else:
            w_in = conv_w_in[j].astype(BF16)
            w_out = conv_w_out[j].astype(BF16)
            z_col = 2
            conv_args = (conv_dw_w[j], conv_dw_b[j], conv_ln_g[j], conv_ln_b[j])
            proj = _in_proj(x2, norm_g[i], shift, scale, w_in, seq, 0)
            a = _conv_module(proj, b * seq, seq, d, *conv_args)
            if with_ctx:
                proj_c = _in_proj(c2, norm_g[i], shift, scale, w_in, b * n_ctx, b)
                a_c = _conv_module(proj_c, b * n_ctx, n_ctx, d, *conv_args)
        x2 = _out_proj(a, proj, z_col, x2, gate, w_out, final_g, seq, 0, final_norm=last)
        if with_ctx:
            c2 = _out_proj(a_c, proj_c, z_col, c2, gate, w_out, final_g, b * n_ctx, b, final_norm=False)
    return x2.reshape(b, seq, d)
```

```python
import functools
import math

import jax
import jax.numpy as jnp
from jax import lax
from jax.experimental import pallas as pl
from jax.experimental.pallas import tpu as pltpu

HEAD_DIM = 64
N_KV_HEADS = 4
FOURIER_GROUPS = 8
WINDOW = 128
GRID_W = 64
N_MIXERS = 3
NORM_EPS = 1e-6
ROPE_BASE = 10000.0
NEG_INF = -1e30
ATTN_SCALE = HEAD_DIM ** -0.5
LOG2_E = math.log2(math.e)

LANES = 128
SUBLANES = 8
MXU_COLS = 256
VMEM_LIMIT_BYTES = 56 * 1024 * 1024

DFT_N1 = 128
IN_PROJ_COL_TILES = 4
INTERLEAVE = 4
SCORE_LOOKAHEAD = 1
ROW_CHUNK = 16
ROW_UNROLL = 4
CONV_HALO = 16

F32 = jnp.float32
BF16 = jnp.bfloat16


def _params(*sem):
    return pltpu.CompilerParams(dimension_semantics=sem, vmem_limit_bytes=VMEM_LIMIT_BYTES)


def _silu(v):
    return v * jax.nn.sigmoid(v)


def _col_pieces(n):
    units = -(-n // MXU_COLS)
    p = min(INTERLEAVE, units)
    bounds = [min(n, (k * units // p) * MXU_COLS) for k in range(p + 1)]
    return list(zip(bounds[:-1], bounds[1:]))


def _pick_tile(n, cap, quantum):
    if n <= cap:
        return n
    t = cap - cap % quantum
    while n % t:
        t -= quantum
    return t


def _mod_kernel(cond_ref, w_ref, b_ref, o_ref):
    s = _silu(cond_ref[...]).astype(BF16)
    o_ref[0] = jnp.dot(s, w_ref[0].astype(BF16), preferred_element_type=F32) + b_ref[0]


def _modulation(cond, ada_w, ada_b):
    depth, d, d3 = ada_w.shape
    r = cond.shape[0]
    tn = _pick_tile(d3, 1024, 2 * LANES)
    return pl.pallas_call(
        _mod_kernel,
        grid=(depth, d3 // tn),
        in_specs=[
            pl.BlockSpec((r, d), lambda l, j: (0, 0)),
            pl.BlockSpec((1, d, tn), lambda l, j: (l, 0, j)),
            pl.BlockSpec((1, 1, tn), lambda l, j: (l, 0, j)),
        ],
        out_specs=pl.BlockSpec((1, r, tn), lambda l, j: (l, 0, j)),
        out_shape=jax.ShapeDtypeStruct((depth, r, d3), F32),
        compiler_params=_params("arbitrary", "arbitrary"),
        name="ada_modulation",
    )(cond, ada_w, ada_b.reshape(depth, 1, d3))


def _in_proj_kernel(x_ref, g_ref, shift_ref, scale_ref, w_ref, o_ref, h0_scr, h1_scr):
    i = pl.program_id(0)
    j = pl.program_id(1)
    r = x_ref.shape[0]

    @pl.when((i == 0) & (j == 0))
    def _():
        h1_scr[...] = jnp.zeros(h1_scr.shape, BF16)

    def step(h_write, h_read):
        g = g_ref[...]
        sh = shift_ref[0]
        sc = 1.0 + scale_ref[0]
        tn = w_ref.shape[1]
        n_chunks = r // ROW_CHUNK
        pieces = _col_pieces(tn)
        for piece, (lo, hi) in enumerate(pieces):
            cols = slice(lo, hi)
            o_ref[:, cols] = jnp.dot(h_read[...], w_ref[:, cols], preferred_element_type=F32).astype(BF16)
            for c in range(piece * n_chunks // len(pieces), (piece + 1) * n_chunks // len(pieces)):
                xf = x_ref[c * ROW_CHUNK:(c + 1) * ROW_CHUNK, :]
                ms = jnp.mean(xf * xf, axis=-1, keepdims=True)
                y = xf * lax.rsqrt(ms + NORM_EPS)
                rows = pl.ds(pl.multiple_of(j * r + c * ROW_CHUNK, ROW_CHUNK), ROW_CHUNK)
                h_write[rows, :] = ((y * g) * sc + sh).astype(BF16)

    @pl.when(i % 2 == 0)
    def _():
        step(h0_scr, h1_scr)

    @pl.when(i % 2 == 1)
    def _():
        step(h1_scr, h0_scr)


def _in_proj(x2, norm_g, shift, scale, w_bf16, rows_per_mod, mod_row0):
    t, d = x2.shape
    n_out = w_bf16.shape[1]
    tm = min(1024, rows_per_mod, t)
    nn = IN_PROJ_COL_TILES
    if (n_out // nn) % MXU_COLS and (n_out // (nn // 2)) % MXU_COLS == 0 and (tm // 2) % (nn // 2 * ROW_CHUNK) == 0:
        tm, nn = tm // 2, nn // 2
    tn = n_out // nn
    r = tm // nn
    assert t % tm == 0 and rows_per_mod % tm == 0 and n_out % nn == 0 and tn % LANES == 0 and r % ROW_CHUNK == 0
    nm = t // tm
    tiles_per_mod = rows_per_mod // tm
    norm_tile = lambda i: jnp.minimum(i, nm - 1)
    mod_map = lambda i, j: (mod_row0 + norm_tile(i) // tiles_per_mod, 0, 0)
    return pl.pallas_call(
        _in_proj_kernel,
        grid=(nm + 1, nn),
        in_specs=[
            pl.BlockSpec((r, d), lambda i, j: (norm_tile(i) * nn + j, 0)),
            pl.BlockSpec((1, d), lambda i, j: (0, 0)),
            pl.BlockSpec((1, 1, d), mod_map),
            pl.BlockSpec((1, 1, d), mod_map),
            pl.BlockSpec((d, tn), lambda i, j: (0, j)),
        ],
        out_specs=pl.BlockSpec((tm, tn), lambda i, j: (jnp.where(i == 0, nm, i - 1), j)),
        out_shape=jax.ShapeDtypeStruct((t + tm, n_out), BF16),
        scratch_shapes=[pltpu.VMEM((tm, d), BF16), pltpu.VMEM((tm, d), BF16)],
        compiler_params=_params("arbitrary", "arbitrary"),
        name="norm_in_proj",
    )(x2, norm_g.reshape(1, d), shift, scale, w_bf16)


def _out_proj_kernel(a_ref, z_ref, x_ref, gate_ref, w_ref, fg_ref, o_ref, y0_scr, y1_scr, *, final_norm, a_run):
    i = pl.program_id(0)
    tm = x_ref.shape[0]

    @pl.when(i == 0)
    def _():
        y1_scr[...] = jnp.zeros(y1_scr.shape, BF16)

    def step(y_write, y_read):
        d = w_ref.shape[1]
        n_chunks = tm // ROW_CHUNK
        gate = gate_ref[0]
        pieces = _col_pieces(d)
        for piece, (lo, hi) in enumerate(pieces):
            cols = slice(lo, hi)
            o = jnp.dot(y_read[...], w_ref[:, cols], preferred_element_type=F32)
            o_ref[:, cols] = x_ref[:, cols] + gate[:, cols] * o
            for c in range(piece * n_chunks // len(pieces), (piece + 1) * n_chunks // len(pieces)):
                rows = slice(c * ROW_CHUNK, (c + 1) * ROW_CHUNK)
                if len(a_ref.shape) == 3:
                    off = (c * ROW_CHUNK) % a_run
                    a = a_ref[(c * ROW_CHUNK) // a_run, off:off + ROW_CHUNK, :]
                else:
                    a = a_ref[rows, :]
                y_write[rows, :] = (a.astype(F32) * _silu(z_ref[rows, :].astype(F32))).astype(BF16)
        if final_norm:
            fg = fg_ref[...]
            for c in range(n_chunks):
                rows = slice(c * ROW_CHUNK, (c + 1) * ROW_CHUNK)
                xn = o_ref[rows, :]
                ms = jnp.mean(xn * xn, axis=-1, keepdims=True)
                o_ref[rows, :] = (xn * lax.rsqrt(ms + NORM_EPS)) * fg

    @pl.when(i % 2 == 0)
    def _():
        step(y0_scr, y1_scr)

    @pl.when(i % 2 == 1)
    def _():
        step(y1_scr, y0_scr)


def _out_proj(a, z_arr, z_col, x2, gate, w_bf16, final_g, rows_per_mod, mod_row0, final_norm):
    t, d = x2.shape
    tm = min(512, rows_per_mod, t)
    assert t % tm == 0 and rows_per_mod % tm == 0
    nm = t // tm
    tiles_per_mod = rows_per_mod // tm
    gate_tile = lambda i: jnp.minimum(i, nm - 1)
    mm_tile = lambda i: jnp.maximum(i - 1, 0)
    a_run = None
    a_spec = pl.BlockSpec((tm, d), lambda i: (gate_tile(i), 0))
    if a.ndim == 3:
        a_run = t // a.shape[0]
        assert tm % a_run == 0 and a_run % ROW_CHUNK == 0
        a_spec = pl.BlockSpec((tm // a_run, a.shape[1], d), lambda i: (gate_tile(i), 0, 0))
    return pl.pallas_call(
        functools.partial(_out_proj_kernel, final_norm=final_norm, a_run=a_run),
        grid=(nm + 1,),
        in_specs=[
            a_spec,
            pl.BlockSpec((tm, d), lambda i: (gate_tile(i), z_col)),
            pl.BlockSpec((tm, d), lambda i: (mm_tile(i), 0)),
            pl.BlockSpec((1, 1, d), lambda i: (mod_row0 + mm_tile(i) // tiles_per_mod, 0, 0)),
            pl.BlockSpec((d, d), lambda i: (0, 0), pipeline_mode=pl.Buffered(1)),
            pl.BlockSpec((1, d), lambda i: (0, 0)),
        ],
        out_specs=pl.BlockSpec((tm, d), lambda i: (mm_tile(i), 0)),
        out_shape=jax.ShapeDtypeStruct((t, d), F32),
        scratch_shapes=[pltpu.VMEM((tm, d), BF16), pltpu.VMEM((tm, d), BF16)],
        compiler_params=_params("arbitrary"),
        name="gated_out_proj",
    )(a, z_arr, x2, gate, w_bf16, final_g.reshape(1, d))


def _cos_sin(n_rows, n_cols, period):
    r = jnp.arange(n_rows, dtype=jnp.int32)[:, None]
    c = jnp.arange(n_cols, dtype=jnp.int32)[None, :]
    ang = ((r * c) % period).astype(F32) * (2.0 * math.pi / period)
    return jnp.cos(ang), jnp.sin(ang)


def _pack_complex(re, im):
    hi = lax.bitcast_convert_type(re.astype(BF16).astype(F32), jnp.uint32)
    lo = lax.bitcast_convert_type(im.astype(BF16).astype(F32), jnp.uint32)
    return hi | (lo >> 16)


def _unpack_complex(w):
    re = lax.bitcast_convert_type(w & jnp.uint32(0xFFFF0000), F32)
    im = lax.bitcast_convert_type(w << 16, F32)
    return re.astype(BF16), im.astype(BF16)


def _chan_dft_kernel(u_ref, m_ref, z_ref, *, groups, run):
    gs = m_ref.shape[0]
    tm = u_ref.shape[0]
    for g in range(groups):
        cols = slice(g * gs, (g + 1) * gs)
        r = jnp.dot(u_ref[:, cols], m_ref[...], preferred_element_type=F32)
        packed = _pack_complex(r[:, :gs], r[:, gs:])
        if run is None:
            z_ref[:, cols] = packed
        else:
            for k in range(tm // run):
                z_ref[k, 0:run, cols] = packed[k * run:(k + 1) * run]
    if run is not None:
        pad = z_ref.shape[1] - run
        z_ref[:, run:, :] = jnp.zeros((tm // run, pad, z_ref.shape[2]), jnp.uint32)


def _chan_dft(proj, t, d, run=None, pitch=None):
    gs = d // FOURIER_GROUPS
    c, s = _cos_sin(gs, gs, gs)
    m = jnp.concatenate([c, -s], axis=1).astype(BF16)
    tm = min(1024, t)
    assert t % tm == 0
    if run is None:
        out_spec = pl.BlockSpec((tm, d), lambda i: (i, 0))
        out_shape = jax.ShapeDtypeStruct((t, d), jnp.uint32)
    else:
        assert tm % run == 0
        out_spec = pl.BlockSpec((tm // run, pitch, d), lambda i: (i, 0, 0))
        out_shape = jax.ShapeDtypeStruct((t // run, pitch, d), jnp.uint32)
    return pl.pallas_call(
        functools.partial(_chan_dft_kernel, groups=FOURIER_GROUPS, run=run),
        grid=(t // tm,),
        in_specs=[
            pl.BlockSpec((tm, d), lambda i: (i, 0)),
            pl.BlockSpec((gs, 2 * gs), lambda i: (0, 0)),
        ],
        out_specs=out_spec,
        out_shape=out_shape,
        compiler_params=_params("arbitrary"),
        name="fourier_chan_dft",
    )(proj, m)


def _stack_pair(parts):
    return jnp.concatenate([jnp.concatenate([parts[0][0], parts[1][0]], axis=1),
                            jnp.concatenate([parts[0][1], parts[1][1]], axis=1)], axis=0)


def _seq_dft_kernel(z_ref, m1_ref, m2_ref, twc_ref, tws_ref, o_ref, y_scr, *, n1, n2, pitch, opitch, inv_norm):
    twc = twc_ref[...]
    tws = tws_ref[...]
    for q in range(0, n2, 2):
        zin = _stack_pair([_unpack_complex(z_ref[0, pl.ds(qq, n1, stride=pitch), :]) for qq in (q, q + 1)])
        y = jnp.dot(m1_ref[...], zin, preferred_element_type=F32)
        for t, qq in enumerate((q, q + 1)):
            yr = y[:n1, t * LANES:(t + 1) * LANES]
            yi = y[n1:, t * LANES:(t + 1) * LANES]
            c = twc[:, qq:qq + 1]
            s = tws[:, qq:qq + 1]
            y_scr[pl.ds(qq, n1, stride=pitch), :] = _pack_complex(yr * c + yi * s, yi * c - yr * s)
    for k in range(0, n1, 2):
        yin = _stack_pair([_unpack_complex(y_scr[kk * pitch:kk * pitch + n2, :]) for kk in (k, k + 1)])
        r = jnp.dot(m2_ref[...], yin, preferred_element_type=F32) * inv_norm
        for t, kk in enumerate((k, k + 1)):
            o_ref[0, pl.ds(kk, n2, stride=opitch), :] = r[:, t * LANES:(t + 1) * LANES]
    for j in range(n1, opitch):
        o_ref[0, pl.ds(j, n2, stride=opitch), :] = jnp.zeros((n2, LANES), F32)


def _seq_dft(zp, b, seq, d, pitch, opitch, inv_norm):
    n1 = DFT_N1
    n2 = seq // n1
    assert seq % n1 == 0 and n1 % 2 == 0 and n2 % 2 == 0 and d % LANES == 0
    c1, s1 = _cos_sin(n1, n1, n1)
    m1 = jnp.concatenate([jnp.concatenate([c1, s1], axis=1),
                          jnp.concatenate([-s1, c1], axis=1)], axis=0).astype(BF16)
    c2, s2 = _cos_sin(n2, n2, n2)
    m2 = jnp.concatenate([c2, s2], axis=1).astype(BF16)
    twc, tws = _cos_sin(n1, n2, seq)
    const = lambda shape: pl.BlockSpec(shape, lambda bi, k: (0, 0))
    out = pl.pallas_call(
        functools.partial(_seq_dft_kernel, n1=n1, n2=n2, pitch=pitch, opitch=opitch, inv_norm=inv_norm),
        grid=(b, d // LANES),
        in_specs=[
            pl.BlockSpec((1, n1 * pitch, LANES), lambda bi, k: (bi, 0, k)),
            const((2 * n1, 2 * n1)), const((n2, 2 * n2)), const((n1, n2)), const((n1, n2)),
        ],
        out_specs=pl.BlockSpec((1, n2 * opitch, LANES), lambda bi, k: (bi, 0, k)),
        out_shape=jax.ShapeDtypeStruct((b, n2 * opitch, d), F32),
        scratch_shapes=[pltpu.VMEM((n1 * pitch, LANES), jnp.uint32)],
        compiler_params=_params("arbitrary", "arbitrary"),
        name="fourier_seq_dft",
    )(zp.reshape(b, n1 * pitch, d), m1, m2, twc, tws)
    return out.reshape(b * n2, opitch, d)


def _seq_dft_direct_kernel(z_ref, m_ref, o_ref, *, inv_norm):
    zr, zi = _unpack_complex(z_ref[0])
    o_ref[0] = jnp.dot(m_ref[...], jnp.concatenate([zr, zi], axis=0), preferred_element_type=F32) * inv_norm


def _seq_dft_direct(z, b, seq, d, inv_norm):
    dt = min(1024, d)
    c, s = _cos_sin(seq, seq, seq)
    m = jnp.concatenate([c, s], axis=1).astype(BF16)
    out = pl.pallas_call(
        functools.partial(_seq_dft_direct_kernel, inv_norm=inv_norm),
        grid=(b, d // dt),
        in_specs=[
            pl.BlockSpec((1, seq, dt), lambda bi, k: (bi, 0, k)),
            pl.BlockSpec((seq, 2 * seq), lambda bi, k: (0, 0)),
        ],
        out_specs=pl.BlockSpec((1, seq, dt), lambda bi, k: (bi, 0, k)),
        out_shape=jax.ShapeDtypeStruct((b, seq, d), F32),
        compiler_params=_params("arbitrary", "arbitrary"),
        name="fourier_seq_dft_direct",
    )(z.reshape(b, seq, d), m)
    return out.reshape(b * seq, d)


def _fourier_mixed(proj, b, seq, d, two_stage):
    inv_norm = 1.0 / math.sqrt(seq * (d // FOURIER_GROUPS))
    if two_stage:
        n2 = seq // DFT_N1
        pitch = n2 + SUBLANES
        z = _chan_dft(proj, b * seq, d, run=n2, pitch=pitch)
        return _seq_dft(z, b, seq, d, pitch, DFT_N1 + SUBLANES, inv_norm)
    return _seq_dft_direct(_chan_dft(proj, b * seq, d), b, seq, d, inv_norm)


def _rope_tables(n_tokens):
    rows = n_tokens // GRID_W
    row = jnp.repeat(jnp.arange(rows), GRID_W).astype(F32)
    col = jnp.tile(jnp.arange(GRID_W), rows).astype(F32)
    quarter = HEAD_DIM // 4
    inv_freq = ROPE_BASE ** (-jnp.arange(quarter, dtype=F32) / quarter)
    ang_r = row[:, None] * inv_freq[None, :]
    ang_c = col[:, None] * inv_freq[None, :]
    ang = jnp.concatenate([ang_r, ang_r, ang_c, ang_c], axis=-1)
    sign = jnp.where((jnp.arange(HEAD_DIM) % (HEAD_DIM // 2)) < quarter, -1.0, 1.0).astype(F32)
    reps = LANES // HEAD_DIM
    return jnp.tile(jnp.cos(ang), (1, reps)), jnp.tile(jnp.sin(ang) * sign, (1, reps))


def _rope128(xf, cos, sin_signed):
    quarter = HEAD_DIM // 4
    lane = lax.broadcasted_iota(jnp.int32, xf.shape, 1)
    from_hi = pltpu.roll(xf, LANES - quarter, axis=1)
    from_lo = pltpu.roll(xf, quarter, axis=1)
    partner = jnp.where((lane % (HEAD_DIM // 2)) < quarter, from_hi, from_lo)
    return xf * cos + partner * sin_signed


def _k_rope_kernel(k_ref, cos_ref, sin_ref, o_ref):
    cos = cos_ref[...]
    sin = sin_ref[...]
    for c in range(k_ref.shape[1] // LANES):
        cols = slice(c * LANES, (c + 1) * LANES)
        o_ref[:, cols] = _rope128(k_ref[:, cols].astype(F32), cos, sin).astype(BF16)


def _k_rope(proj, t, k_col, kvw, seq, cos, sin):
    tm = min(1024, seq)
    assert seq % tm == 0 and kvw % LANES == 0
    tiles_per_seq = seq // tm
    return pl.pallas_call(
        _k_rope_kernel,
        grid=(t // tm,),
        in_specs=[
            pl.BlockSpec((tm, kvw), lambda i: (i, k_col)),
            pl.BlockSpec((tm, LANES), lambda i: (i % tiles_per_seq, 0)),
            pl.BlockSpec((tm, LANES), lambda i: (i % tiles_per_seq, 0)),
        ],
        out_specs=pl.BlockSpec((tm, kvw), lambda i: (i, 0)),
        out_shape=jax.ShapeDtypeStruct((t, kvw), BF16),
        compiler_params=_params("arbitrary"),
        name="attn_k_rope",
    )(proj, cos, sin)


def _head_halves(cat_f32, lo):
    lane = lax.broadcasted_iota(jnp.int32, cat_f32.shape, 1)
    same = jnp.where((lane >= lo) & (lane < lo + HEAD_DIM), cat_f32, 0.0)
    swap = pltpu.roll(same, HEAD_DIM, axis=1)
    low, high = (same, swap) if lo == 0 else (swap, same)
    return low.astype(BF16), high.astype(BF16)


def _attn_kernel(*refs, band, bq, seq, n_ctx, groups):
    if band:
        (sink_ref, q_ref, cos_ref, sin_ref, kp_ref, kc_ref, kn_ref,
         vp_ref, vc_ref, vn_ref, kx_ref, vx_ref, o_ref, cap_scr) = refs
    else:
        sink_ref, q_ref, kx_ref, vx_ref, o_ref = refs
    n_band = bq + 2 * WINDOW if band else 0
    nk = n_band + n_ctx
    if band:
        start = pl.program_id(1) * bq
        qpos = start + lax.broadcasted_iota(jnp.int32, (bq, nk), 0)
        col = lax.broadcasted_iota(jnp.int32, (bq, nk), 1)
        kpos = start - WINDOW + col
        valid = ((jnp.abs(qpos - kpos) <= WINDOW) & (kpos >= 0) & (kpos < seq)) | (col >= n_band)
        cap_scr[...] = jnp.where(valid, jnp.inf, NEG_INF)
        cos = cos_ref[...]
        sin = sin_ref[...]
    lane_q = lax.broadcasted_iota(jnp.int32, (bq, LANES), 1)
    nt = (((1,), (1,)), ((), ()))

    def probs(s, sink2):
        if band:
            s = jnp.minimum(s, cap_scr[...])
        m = jnp.maximum(jnp.max(s, axis=-1, keepdims=True), sink2)
        p = jnp.exp2(s - m)
        denom = jnp.sum(p, axis=-1, keepdims=True) + jnp.exp2(sink2 - m)
        return p.astype(BF16), 1.0 / denom

    def kv_halves(h):
        pair_cols = slice((h // 2) * LANES, (h // 2 + 1) * LANES)
        k_refs = (kp_ref, kc_ref, kn_ref, kx_ref) if band else (kx_ref,)
        v_refs = (vp_ref, vc_ref, vn_ref, vx_ref) if band else (vx_ref,)
        kcat = jnp.concatenate([r[:, pair_cols].astype(F32) for r in k_refs], axis=0)
        vcat = jnp.concatenate([r[:, pair_cols].astype(F32) for r in v_refs], axis=0)
        lo = (h % 2) * HEAD_DIM
        return _head_halves(kcat, lo) + _head_halves(vcat, lo)

    def scores(h, p, kv):
        off = (h * groups + 2 * p) * HEAD_DIM
        qf = q_ref[:, off:off + LANES].astype(F32)
        if band:
            qf = _rope128(qf, cos, sin)
        qb = (qf * (ATTN_SCALE * LOG2_E)).astype(BF16)
        return (lax.dot_general(qb, kv[0], nt, preferred_element_type=F32),
                lax.dot_general(qb, kv[1], nt, preferred_element_type=F32))

    def finish(h, p, kv, s0, s1):
        off = (h * groups + 2 * p) * HEAD_DIM
        p0, inv0 = probs(s0, sink_ref[h * groups + 2 * p] * LOG2_E)
        p1, inv1 = probs(s1, sink_ref[h * groups + 2 * p + 1] * LOG2_E)
        o_pair = (jnp.dot(p0, kv[2], preferred_element_type=F32)
                  + jnp.dot(p1, kv[3], preferred_element_type=F32))
        o_ref[:, off:off + LANES] = (o_pair * jnp.where(lane_q < HEAD_DIM, inv0, inv1)).astype(BF16)

    pairs = [(h, p) for h in range(N_KV_HEADS) for p in range(groups // 2)]
    kv = {}
    ready = []

    def issue(idx):
        hn, pn = pairs[idx]
        if hn not in kv:
            kv[hn] = kv_halves(hn)
        ready.append(scores(hn, pn, kv[hn]))

    for idx in range(min(SCORE_LOOKAHEAD, len(pairs))):
        issue(idx)
    for idx, (h, p) in enumerate(pairs):
        if idx + SCORE_LOOKAHEAD < len(pairs):
            issue(idx + SCORE_LOOKAHEAD)
        finish(h, p, kv[h], *ready[idx])


def _attention(sink, proj, q_col, v_col, k_rot, proj_c, kc_col, vc_col, b, seq, n_ctx, d, cos, sin):
    kvw = N_KV_HEADS * HEAD_DIM
    groups = d // kvw
    bq = min(256, seq)
    assert seq % bq == 0 and bq % WINDOW == 0 and n_ctx % WINDOW == 0 and groups % 2 == 0 and kvw % LANES == 0
    nq = seq // bq
    wpb = bq // WINDOW
    n_halo = b * seq // WINDOW

    def prev_map(col):
        return lambda bi, i: (jnp.maximum((bi * nq + i) * wpb - 1, 0), col)

    def next_map(col):
        return lambda bi, i: (jnp.minimum((bi * nq + i + 1) * wpb, n_halo - 1), col)

    def cur_map(col):
        return lambda bi, i: (bi * nq + i, col)

    return pl.pallas_call(
        functools.partial(_attn_kernel, band=True, bq=bq, seq=seq, n_ctx=n_ctx, groups=groups),
        grid=(b, nq),
        in_specs=[
            pl.BlockSpec(memory_space=pltpu.SMEM),
            pl.BlockSpec((bq, d), cur_map(q_col)),
            pl.BlockSpec((bq, LANES), lambda bi, i: (i, 0)),
            pl.BlockSpec((bq, LANES), lambda bi, i: (i, 0)),
            pl.BlockSpec((WINDOW, kvw), prev_map(0)),
            pl.BlockSpec((bq, kvw), cur_map(0)),
            pl.BlockSpec((WINDOW, kvw), next_map(0)),
            pl.BlockSpec((WINDOW, kvw), prev_map(v_col)),
            pl.BlockSpec((bq, kvw), cur_map(v_col)),
            pl.BlockSpec((WINDOW, kvw), next_map(v_col)),
            pl.BlockSpec((n_ctx, kvw), lambda bi, i: (bi, kc_col)),
            pl.BlockSpec((n_ctx, kvw), lambda bi, i: (bi, vc_col)),
        ],
        out_specs=pl.BlockSpec((bq, d), lambda bi, i: (bi * nq + i, 0)),
        out_shape=jax.ShapeDtypeStruct((b * seq, d), BF16),
        scratch_shapes=[pltpu.VMEM((bq, bq + 2 * WINDOW + n_ctx), F32)],
        compiler_params=_params("arbitrary", "arbitrary"),
        name="window_attention",
    )(sink, proj, cos, sin, k_rot, k_rot, k_rot, proj, proj, proj, proj_c, proj_c)


def _ctx_attention(sink, proj_c, q_col, k_col, v_col, b, n_ctx, d):
    kvw = N_KV_HEADS * HEAD_DIM
    groups = d // kvw
    return pl.pallas_call(
        functools.partial(_attn_kernel, band=False, bq=n_ctx, seq=n_ctx, n_ctx=n_ctx, groups=groups),
        grid=(b,),
        in_specs=[
            pl.BlockSpec(memory_space=pltpu.SMEM),
            pl.BlockSpec((n_ctx, d), lambda bi: (bi, q_col)),
            pl.BlockSpec((n_ctx, kvw), lambda bi: (bi, k_col)),
            pl.BlockSpec((n_ctx, kvw), lambda bi: (bi, v_col)),
        ],
        out_specs=pl.BlockSpec((n_ctx, d), lambda bi: (bi, 0)),
        out_shape=jax.ShapeDtypeStruct((b * n_ctx, d), BF16),
        compiler_params=_params("arbitrary"),
        name="context_attention",
    )(sink, proj_c, proj_c, proj_c)


def _conv_kernel(a_ref, ag_ref, ap_ref, agp_ref, an_ref, agn_ref, w_ref, b_ref, lg_ref, lb_ref,
                 o_ref, g_scr, sh_scr, c_scr, wb_scr, *, tiles_per_seq, kw):
    tm, d = a_ref.shape
    halo = CONV_HALO
    span = tm + 2 * halo
    i = pl.program_id(0)
    first = (i % tiles_per_seq) == 0
    last = (i % tiles_per_seq) == tiles_per_seq - 1

    def glu(a, ag):
        return a.astype(F32) * jax.nn.sigmoid(ag.astype(F32))

    g_scr[0:halo, :] = jnp.where(first, 0.0, glu(ap_ref[...], agp_ref[...]))

    def glu_body(r, carry):
        rows = pl.ds(pl.multiple_of(r * ROW_CHUNK, ROW_CHUNK), ROW_CHUNK)
        g_scr[pl.ds(pl.multiple_of(halo + r * ROW_CHUNK, ROW_CHUNK), ROW_CHUNK), :] = glu(a_ref[rows, :], ag_ref[rows, :])
        return carry

    lax.fori_loop(0, tm // ROW_CHUNK, glu_body, 0, unroll=ROW_UNROLL)
    g_scr[halo + tm:span, :] = jnp.where(last, 0.0, glu(an_ref[...], agn_ref[...]))
    g_scr[span:span + SUBLANES, :] = jnp.zeros((SUBLANES, d), F32)

    for r in range(1, SUBLANES):
        sh_scr[r - 1] = g_scr[r:r + span, :]

    def shifted(r, rows, cols):
        return g_scr[rows, cols] if r == 0 else sh_scr[r - 1, rows, cols]

    pad = kw // 2
    lw = 4 * LANES
    bias = b_ref[...]
    lg = lg_ref[...]
    lb = lb_ref[...]

    @pl.when(i == 0)
    def _():
        for j in range(kw):
            wb_scr[j] = jnp.broadcast_to(w_ref[j:j + 1, :], (SUBLANES, d))

    groups = ROW_CHUNK // SUBLANES

    def row_body(rc, carry):
        base = pl.multiple_of(rc * ROW_CHUNK, ROW_CHUNK)
        for ch in range(d // lw):
            cols = slice(ch * lw, (ch + 1) * lw)
            acc = [jnp.zeros((SUBLANES, lw), F32) for _ in range(groups)]
            for j in range(kw):
                off = j + halo - pad
                w = wb_scr[j, :, cols]
                for s in range(groups):
                    rows = pl.ds(pl.multiple_of(base + (off // SUBLANES + s) * SUBLANES, SUBLANES), SUBLANES)
                    acc[s] = acc[s] + w * shifted(off % SUBLANES, rows, cols)
            for s in range(groups):
                rows = pl.ds(pl.multiple_of(base + s * SUBLANES, SUBLANES), SUBLANES)
                c_scr[rows, cols] = acc[s] + bias[:, cols]
        y = c_scr[pl.ds(base, ROW_CHUNK), :]
        mu = jnp.mean(y, axis=-1, keepdims=True)
        var = jnp.mean(jnp.square(y - mu), axis=-1, keepdims=True)
        yn = (y - mu) * lax.rsqrt(var + NORM_EPS) * lg + lb
        o_ref[pl.ds(base, ROW_CHUNK), :] = _silu(yn).astype(BF16)
        return carry

    lax.fori_loop(0, tm // ROW_CHUNK, row_body, 0, unroll=4)


def _conv_module(proj, t, seq, d, dw_w, dw_b, ln_g, ln_b):
    kw = dw_w.shape[0]
    tm = min(256, seq)
    halo = CONV_HALO
    assert seq % tm == 0 and tm % halo == 0 and kw // 2 <= halo and d % (4 * LANES) == 0
    tiles_per_seq = seq // tm
    hpt = tm // halo
    n_halo = t // halo
    span = tm + 2 * halo

    def prev_map(col):
        return lambda i: (jnp.maximum(i * hpt - 1, 0), col)

    def next_map(col):
        return lambda i: (jnp.minimum((i + 1) * hpt, n_halo - 1), col)

    vec = lambda: pl.BlockSpec((1, d), lambda i: (0, 0))
    return pl.pallas_call(
        functools.partial(_conv_kernel, tiles_per_seq=tiles_per_seq, kw=kw),
        grid=(t // tm,),
        in_specs=[
            pl.BlockSpec((tm, d), lambda i: (i, 0)),
            pl.BlockSpec((tm, d), lambda i: (i, 1)),
            pl.BlockSpec((halo, d), prev_map(0)),
            pl.BlockSpec((halo, d), prev_map(1)),
            pl.BlockSpec((halo, d), next_map(0)),
            pl.BlockSpec((halo, d), next_map(1)),
            pl.BlockSpec((kw, d), lambda i: (0, 0)),
            vec(), vec(), vec(),
        ],
        out_specs=pl.BlockSpec((tm, d), lambda i: (i, 0)),
        out_shape=jax.ShapeDtypeStruct((t, d), BF16),
        scratch_shapes=[
            pltpu.VMEM((span + SUBLANES, d), F32),
            pltpu.VMEM((SUBLANES - 1, span, d), F32),
            pltpu.VMEM((tm, d), F32),
            pltpu.VMEM((kw, SUBLANES, d), F32),
        ],
        compiler_params=_params("arbitrary"),
        name="conformer_conv",
    )(proj, proj, proj, proj, proj, proj, dw_w, dw_b.reshape(1, d), ln_g.reshape(1, d), ln_b.reshape(1, d))


def kernel(x, c, ctx, c_ctx, norm_g, ada_w, ada_b, four_w_in, four_w_out, attn_w_in, attn_sink,
           attn_w_out, conv_w_in, conv_dw_w, conv_dw_b, conv_ln_g, conv_ln_b, conv_w_out, final_g):
    b, seq, d = x.shape
    n_ctx = ctx.shape[1]
    depth = norm_g.shape[0]
    kvw = N_KV_HEADS * HEAD_DIM
    assert d % kvw == 0 and seq % GRID_W == 0

    n_rows = -(-(b + 1) // SUBLANES) * SUBLANES
    cond = jnp.concatenate([c, c_ctx[None, :], jnp.zeros((n_rows - b - 1, d), F32)], axis=0)
    mod = _modulation(cond, ada_w, ada_b)

    cos, sin = _rope_tables(seq)
    qe = d
    ke = qe + kvw
    ve = ke + kvw
    attn_w_perm = jnp.concatenate([attn_w_in[:, :, ve:], attn_w_in[:, :, :ve]], axis=-1)

    x2 = x.reshape(b * seq, d)
    c2 = ctx.reshape(b * n_ctx, d)
    for i in range(depth):
        kind, j = i % N_MIXERS, i // N_MIXERS
        with_ctx = i < depth - 1
        last = i == depth - 1
        shift = mod[i, :, 0:d].reshape(n_rows, 1, d)
        scale = mod[i, :, d:2 * d].reshape(n_rows, 1, d)
        gate = mod[i, :, 2 * d:3 * d].reshape(n_rows, 1, d)
        if kind == 0:
            w_in = four_w_in[j].astype(BF16)
            w_out = four_w_out[j].astype(BF16)
            proj = _in_proj(x2, norm_g[i], shift, scale, w_in, seq, 0)
            a = _fourier_mixed(proj, b, seq, d, two_stage=True)
            z_col = 1
            if with_ctx:
                proj_c = _in_proj(c2, norm_g[i], shift, scale, w_in, b * n_ctx, b)
                a_c = _fourier_mixed(proj_c, b, n_ctx, d, two_stage=False)
        elif kind == 1:
            w_in = attn_w_perm[j].astype(BF16)
            w_out = attn_w_out[j].astype(BF16)
            z_col, q_col = 0, 1
            k_col = 2 * d // kvw
            v_col = k_col + 1
            proj = _in_proj(x2, norm_g[i], shift, scale, w_in, seq, 0)
            proj_c = _in_proj(c2, norm_g[i], shift, scale, w_in, b * n_ctx, b)
            k_rot = _k_rope(proj, b * seq, k_col, kvw, seq, cos, sin)
            a = _attention(attn_sink[j], proj, q_col, v_col, k_rot, proj_c, k_col, v_col,
                           b, seq, n_ctx, d, cos, sin)
            if with_ctx:
                a_c = _ctx_attention(attn_sink[j], proj_c, q_col, k_col, v_col, b, n_ctx, d)
        else:
            w_in = conv_w_in[j].astype(BF16)
            w_out = conv_w_out[j].astype(BF16)
            z_col = 2
            conv_args = (conv_dw_w[j], conv_dw_b[j], conv_ln_g[j], conv_ln_b[j])
            proj = _in_proj(x2, norm_g[i], shift, scale, w_in, seq, 0)
            a = _conv_module(proj, b * seq, seq, d, *conv_args)
            if with_ctx:
                proj_c = _in_proj(c2, norm_g[i], shift, scale, w_in, b * n_ctx, b)
                a_c = _conv_module(proj_c, b * n_ctx, n_ctx, d, *conv_args)
        x2 = _out_proj(a, proj, z_col, x2, gate, w_out, final_g, seq, 0, final_norm=last)
        if with_ctx:
            c2 = _out_proj(a_c, proj_c, z_col, c2, gate, w_out, final_g, b * n_ctx, b, final_norm=False)
    return x2.reshape(b, seq, d)
```
